```python
import math
import jax, jax.numpy as jnp
from jax import lax
import numpy as np

D_MODEL = 4096
BATCH = 16
SEQ = 256
DEPTH = 4
DEC_BATCH = 8
DEC_SEQ = 4096
PAST_LEN = 256

GRID_W = 64
HEAD_DIM = 128
N_BRANCH = 3
BRANCH_W = D_MODEL // 2
A_HEADS = BRANCH_W // HEAD_DIM
CONV_K = 3
CHUNK = 64
B_HEADS = BRANCH_W // HEAD_DIM
B_KV = 4
WINDOW = 128
BLOCK = 128
C_HEADS = BRANCH_W // HEAD_DIM
C_KV = 4
Q_BLOCK = 128
KV_W = B_KV * HEAD_DIM
ROPE_THETA = 10000.0
EPS = 1e-6
NEG = -1e30
IN_WIDTHS = (3 * BRANCH_W, BRANCH_W, 4 * A_HEADS, BRANCH_W, KV_W, KV_W, BRANCH_W, BRANCH_W, KV_W, KV_W, BRANCH_W, N_BRANCH * D_MODEL)
IN_WIDTH = 3 * BRANCH_W + BRANCH_W + 4 * A_HEADS + 2 * (2 * BRANCH_W + 2 * KV_W) + N_BRANCH * D_MODEL

kernel_name = "hybrid_diffusion_prefix_deltanet_window_axial"


def _in_offsets():
    return [int(o) for o in np.cumsum(np.array(IN_WIDTHS))[:-1]]


def _rmsnorm(x, g):
    xf = x.astype(jnp.float32)
    y = xf * lax.rsqrt(jnp.mean(xf * xf, axis=-1, keepdims=True) + EPS)
    return (y * g.astype(jnp.float32)).astype(x.dtype)


def _l2norm(x):
    xf = x.astype(jnp.float32)
    return (xf * lax.rsqrt(jnp.sum(xf * xf, axis=-1, keepdims=True) + EPS)).astype(x.dtype)


def _heads(t, n):
    return t.reshape(t.shape[0], t.shape[1], n, HEAD_DIM)


def _axial_rope(n_tokens):
    rows = n_tokens // GRID_W
    row = jnp.repeat(jnp.arange(rows), GRID_W).astype(jnp.float32)
    col = jnp.tile(jnp.arange(GRID_W), rows).astype(jnp.float32)
    n_freq = HEAD_DIM // 4
    inv = ROPE_THETA ** (-jnp.arange(n_freq, dtype=jnp.float32) / n_freq)
    ang = jnp.concatenate([row[:, None] * inv, col[:, None] * inv], axis=-1)
    return jnp.cos(ang), jnp.sin(ang)


def _apply_rope(x, cos, sin):
    xf = x.astype(jnp.float32)
    x1, x2 = jnp.split(xf, 2, axis=-1)
    c = cos[None, :, None, :]
    s = sin[None, :, None, :]
    return jnp.concatenate([x1 * c - x2 * s, x1 * s + x2 * c], axis=-1).astype(x.dtype)


def _short_conv(x, w):
    ch = x.shape[-1]
    y = lax.conv_general_dilated(x, w[:, None, :].astype(x.dtype), window_strides=(1,),
                                 padding=[(CONV_K // 2, CONV_K // 2)],
                                 dimension_numbers=('NWC', 'WIO', 'NWC'), feature_group_count=ch)
    return jax.nn.silu(y)


def _gated_delta_chunked(q, k, v, g, beta, s0):
    f32 = jnp.float32
    bn, t, h, dk = q.shape
    n = t // CHUNK

    def chunks(a):
        a = a.astype(f32).reshape(bn, n, CHUNK, h, -1)
        return jnp.transpose(a, (1, 0, 3, 2, 4))

    q = chunks(q) * (dk ** -0.5)
    k = chunks(k)
    v = chunks(v)
    dv = v.shape[-1]
    g = jnp.cumsum(chunks(g[..., None])[..., 0], axis=-1)
    beta = chunks(beta[..., None])[..., 0]
    idx = jnp.arange(CHUNK)
    causal = idx[:, None] >= idx[None, :]
    strict = idx[:, None] > idx[None, :]
    decay = jnp.exp(jnp.where(causal, g[..., :, None] - g[..., None, :], -jnp.inf))
    kb = k * beta[..., None]
    a_mat = jnp.where(strict, jnp.einsum('nbhik,nbhjk->nbhij', kb, k) * decay, 0.0)
    lhs = a_mat + jnp.eye(CHUNK, dtype=f32)
    rhs = jnp.concatenate([v * beta[..., None], kb * jnp.exp(g)[..., None]], axis=-1)
    sol = lax.linalg.triangular_solve(lhs, rhs, left_side=True, lower=True, unit_diagonal=True)
    u, w = sol[..., :dv], sol[..., dv:]
    qk = jnp.einsum('nbhik,nbhjk->nbhij', q, k) * decay
    q_dec = q * jnp.exp(g)[..., None]
    k_dec = k * jnp.exp(g[..., -1:] - g)[..., None]
    g_tot = jnp.exp(g[..., -1])

    def step(s, inp):
        qk_i, qd_i, kd_i, u_i, w_i, gt_i = inp
        v_new = u_i - jnp.einsum('bhck,bhkv->bhcv', w_i, s)
        o = jnp.einsum('bhck,bhkv->bhcv', qd_i, s) + jnp.einsum('bhij,bhjv->bhiv', qk_i, v_new)
        s = s * gt_i[..., None, None] + jnp.einsum('bhck,bhcv->bhkv', kd_i, v_new)
        return s, o

    s, o = lax.scan(step, s0.astype(f32), (qk, q_dec, k_dec, u, w, g_tot))
    o = jnp.transpose(o, (1, 0, 3, 2, 4)).reshape(bn, t, h, dv)
    return o, s


def _deltanet(a_qkv, a_ba, conv_w, a_log, dt_bias, s0_f, s0_b):
    bn, t, _ = a_qkv.shape
    q, k, v = jnp.split(_short_conv(a_qkv, conv_w), 3, axis=-1)
    q = _l2norm(_heads(q, A_HEADS))
    k = _l2norm(_heads(k, A_HEADS))
    v = _heads(v, A_HEADS)
    ba = a_ba.astype(jnp.float32).reshape(bn, t, 2, 2, A_HEADS)
    beta = jax.nn.sigmoid(ba[:, :, 0])
    g = -jnp.exp(a_log.astype(jnp.float32)) * jax.nn.softplus(ba[:, :, 1] + dt_bias.astype(jnp.float32))
    o_f, s_f = _gated_delta_chunked(q, k, v, g[:, :, 0], beta[:, :, 0], s0_f)
    rev = lambda a: jnp.flip(a, axis=1)
    o_b, s_b = _gated_delta_chunked(rev(q), rev(k), rev(v), rev(g[:, :, 1]), rev(beta[:, :, 1]), s0_b)
    return o_f + rev(o_b), s_f, s_b


def _deltanet_out(o, norm_a, z):
    bn, t = o.shape[0], o.shape[1]
    return _rmsnorm(o, norm_a).reshape(bn, t, BRANCH_W).astype(z.dtype) * jax.nn.silu(z)


def _sink_softmax(s, sink):
    sink = jnp.broadcast_to(sink.astype(jnp.float32), s.shape[:-1] + (1,))
    m = jnp.maximum(jnp.max(s, axis=-1, keepdims=True), sink)
    p = jnp.exp(s - m)
    return p / (jnp.sum(p, axis=-1, keepdims=True) + jnp.exp(sink - m))


def _dense_attn(q, k, v, sink):
    bn, s_len, h, d = q.shape
    kv = k.shape[2]
    qg = q.reshape(bn, s_len, kv, h // kv, d)
    s = jnp.einsum('bqhgd,bkhd->bhgqk', qg, k).astype(jnp.float32) * (d ** -0.5)
    if sink is None:
        p = jax.nn.softmax(s, axis=-1)
    else:
        p = _sink_softmax(s, sink.reshape(kv, h // kv)[None, :, :, None, None])
    o = jnp.einsum('bhgqk,bkhd->bqhgd', p.astype(v.dtype), v)
    return o.reshape(bn, s_len, h * d)


def _window_attn_latent(q, k, v, kc, vc, sink):
    bn, t, h, d = q.shape
    nb = t // BLOCK
    grp = h // B_KV
    qb = q.reshape(bn, nb, BLOCK, B_KV, grp, d)
    pad = ((0, 0), (BLOCK, BLOCK), (0, 0), (0, 0))
    kp = jnp.pad(k, pad).reshape(bn, nb + 2, BLOCK, B_KV, d)
    vp = jnp.pad(v, pad).reshape(bn, nb + 2, BLOCK, B_KV, d)
    kband = jnp.concatenate([kp[:, :-2], kp[:, 1:-1], kp[:, 2:]], axis=2)
    vband = jnp.concatenate([vp[:, :-2], vp[:, 1:-1], vp[:, 2:]], axis=2)
    qpos = jnp.arange(nb)[:, None] * BLOCK + jnp.arange(BLOCK)[None, :]
    kpos = jnp.arange(nb)[:, None] * BLOCK - BLOCK + jnp.arange(3 * BLOCK)[None, :]
    valid = ((jnp.abs(qpos[:, :, None] - kpos[:, None, :]) <= WINDOW)
             & (kpos[:, None, :] >= 0) & (kpos[:, None, :] < t))
    scale = d ** -0.5
    s_loc = jnp.einsum('bnqhgd,bnkhd->bnhgqk', qb, kband).astype(jnp.float32) * scale
    s_loc = jnp.where(valid[None, :, None, None], s_loc, NEG)
    s_ctx = jnp.einsum('bnqhgd,bshd->bnhgqs', qb, kc).astype(jnp.float32) * scale
    p = _sink_softmax(jnp.concatenate([s_loc, s_ctx], axis=-1),
                      sink.reshape(B_KV, grp)[None, None, :, :, None, None])
    p = p.astype(v.dtype)
    o = (jnp.einsum('bnhgqk,bnkhd->bnqhgd', p[..., :3 * BLOCK], vband)
         + jnp.einsum('bnhgqs,bshd->bnqhgd', p[..., 3 * BLOCK:], vc))
    return o.reshape(bn, t, h * d)


def _global_attn_latent(q, k, v, kc, vc):
    bn, t, h, d = q.shape
    nb = t // Q_BLOCK
    grp = h // C_KV
    kall = jnp.concatenate([k, kc], axis=1)
    vall = jnp.concatenate([v, vc], axis=1)
    qb = jnp.transpose(q.reshape(bn, nb, Q_BLOCK, C_KV, grp, d), (1, 0, 2, 3, 4, 5))

    def one_block(qi):
        s = jnp.einsum('bqhgd,bkhd->bhgqk', qi, kall).astype(jnp.float32) * (d ** -0.5)
        p = jax.nn.softmax(s, axis=-1)
        return jnp.einsum('bhgqk,bkhd->bqhgd', p.astype(vall.dtype), vall)

    o = lax.map(one_block, qb)
    return jnp.transpose(o, (1, 0, 2, 3, 4, 5)).reshape(bn, t, h * d)


def _prep(x, mod, norm_g, w_in):
    shift, scale, gate = jnp.split(mod, 3, axis=-1)
    h = _rmsnorm(x, norm_g) * (1 + scale) + shift
    return gate, jnp.split(h @ w_in, _in_offsets(), axis=-1)


def _merge(branches, gates, w_branch, w_out):
    gs = jnp.split(gates, N_BRANCH, axis=-1)
    mixed = sum(jax.nn.sigmoid(gs[i]) * (branches[i] @ w_branch[i]) for i in range(N_BRANCH))
    return mixed @ w_out


def _context_layer(x, mod, norm_g, w_in, conv_w, a_log, dt_bias, norm_a, sink, q_norm, k_norm, w_branch, w_out):
    gate, parts = _prep(x, mod, norm_g, w_in)
    a_qkv, a_z, a_ba, b_q, b_k, b_v, b_z, c_q, c_k, c_v, c_z, gates = parts
    zero = jnp.zeros((x.shape[0], A_HEADS, HEAD_DIM, HEAD_DIM), jnp.float32)
    oa, s_f, s_b = _deltanet(a_qkv, a_ba, conv_w, a_log, dt_bias, zero, zero)
    ya = _deltanet_out(oa, norm_a, a_z)
    kb, vb = _heads(b_k, B_KV), _heads(b_v, B_KV)
    yb = _dense_attn(_heads(b_q, B_HEADS), kb, vb, sink) * jax.nn.silu(b_z)
    qc = _rmsnorm(_heads(c_q, C_HEADS), q_norm)
    kc = _rmsnorm(_heads(c_k, C_KV), k_norm)
    vc = _heads(c_v, C_KV)
    yc = _dense_attn(qc, kc, vc, None) * jax.nn.silu(c_z)
    x = x + gate * _merge((ya, yb, yc), gates, w_branch, w_out)
    state = jnp.stack([s_f, s_b], axis=1).astype(x.dtype)
    return x, state, kb, vb, kc, vc


def _latent_layer(x, mod, cos, sin, st, wk, wv, gk, gv, norm_g, w_in, conv_w, a_log, dt_bias, norm_a, sink,
                  q_norm, k_norm, w_branch, w_out):
    gate, parts = _prep(x, mod, norm_g, w_in)
    a_qkv, a_z, a_ba, b_q, b_k, b_v, b_z, c_q, c_k, c_v, c_z, gates = parts
    oa, _, _ = _deltanet(a_qkv, a_ba, conv_w, a_log, dt_bias, st[:, 0], st[:, 1])
    ya = _deltanet_out(oa, norm_a, a_z)
    qb = _apply_rope(_heads(b_q, B_HEADS), cos, sin)
    kb = _apply_rope(_heads(b_k, B_KV), cos, sin)
    yb = _window_attn_latent(qb, kb, _heads(b_v, B_KV), wk, wv, sink) * jax.nn.silu(b_z)
    qc = _apply_rope(_rmsnorm(_heads(c_q, C_HEADS), q_norm), cos, sin)
    kc = _apply_rope(_rmsnorm(_heads(c_k, C_KV), k_norm), cos, sin)
    yc = _global_attn_latent(qc, kc, _heads(c_v, C_KV), gk, gv) * jax.nn.silu(c_z)
    return x + gate * _merge((ya, yb, yc), gates, w_branch, w_out)


def setup_inputs(seed: int = 0) -> dict:
    key = jax.random.key(seed)
    ks = jax.random.split(key, 24)
    f32 = jnp.float32

    def nrm(k, shape, s):
        return jax.random.normal(k, shape, f32) * s

    dt = jnp.exp(jax.random.uniform(ks[13], (DEPTH, 2, A_HEADS), f32, math.log(1e-3), math.log(1e-1)))
    return {
        "x_prompt": nrm(ks[0], (BATCH, SEQ, D_MODEL), 1.0),
        "x_sample": nrm(ks[1], (DEC_BATCH, DEC_SEQ, D_MODEL), 1.0),
        "state_delta": nrm(ks[2], (DEC_BATCH, DEPTH, 2, A_HEADS, HEAD_DIM, HEAD_DIM), 0.05),
        "cache_win_k": nrm(ks[3], (DEC_BATCH, DEPTH, PAST_LEN, B_KV, HEAD_DIM), 1.0),
        "cache_win_v": nrm(ks[4], (DEC_BATCH, DEPTH, PAST_LEN, B_KV, HEAD_DIM), 1.0),
        "cache_glob_k": nrm(ks[5], (DEC_BATCH, DEPTH, PAST_LEN, C_KV, HEAD_DIM), 1.0),
        "cache_glob_v": nrm(ks[6], (DEC_BATCH, DEPTH, PAST_LEN, C_KV, HEAD_DIM), 1.0),
        "c": nrm(ks[7], (DEC_BATCH, D_MODEL), 1.0),
        "c_ctx": nrm(ks[8], (D_MODEL,), 1.0),
        "norm_g": 1.0 + nrm(ks[9], (DEPTH, D_MODEL), 0.02),
        "w_mod": nrm(ks[10], (DEPTH, D_MODEL, 3 * D_MODEL), 0.5 * D_MODEL ** -0.5),
        "b_mod": nrm(ks[11], (DEPTH, 3 * D_MODEL), 0.01),
        "w_in": nrm(ks[12], (DEPTH, D_MODEL, IN_WIDTH), D_MODEL ** -0.5),
        "conv_w": nrm(ks[14], (DEPTH, CONV_K, 3 * BRANCH_W), CONV_K ** -0.5),
        "a_log": jnp.log(jax.random.uniform(ks[15], (DEPTH, 2, A_HEADS), f32, 1.0, 16.0)),
        "dt_bias": dt + jnp.log(-jnp.expm1(-dt)),
        "norm_a": 1.0 + nrm(ks[16], (DEPTH, HEAD_DIM), 0.02),
        "sink": nrm(ks[17], (DEPTH, B_HEADS), 0.5),
        "q_norm": 1.0 + nrm(ks[18], (DEPTH, HEAD_DIM), 0.02),
        "k_norm": 1.0 + nrm(ks[19], (DEPTH, HEAD_DIM), 0.02),
        "w_branch": nrm(ks[20], (DEPTH, N_BRANCH, BRANCH_W, D_MODEL), BRANCH_W ** -0.5),
        "w_out": nrm(ks[21], (DEPTH, D_MODEL, D_MODEL), D_MODEL ** -0.5),
        "final_g": 1.0 + nrm(ks[22], (D_MODEL,), 0.02),
    }


def reference(x_prompt, x_sample, state_delta, cache_win_k, cache_win_v, cache_glob_k, cache_glob_v, c,
              c_ctx, norm_g, w_mod, b_mod, w_in, conv_w, a_log, dt_bias, norm_a, sink, q_norm, k_norm,
              w_branch, w_out, final_g):
    cos, sin = _axial_rope(x_sample.shape[1])
    xp, xs = x_prompt, x_sample
    st_l, wk_l, wv_l, gk_l, gv_l = [], [], [], [], []
    for l in range(DEPTH):
        mod_ctx = (jax.nn.silu(c_ctx) @ w_mod[l] + b_mod[l])[None, None, :]
        mod_lat = (jax.nn.silu(c) @ w_mod[l] + b_mod[l])[:, None, :]
        xp, st, kb, vb, kc, vc = _context_layer(xp, mod_ctx, norm_g[l], w_in[l], conv_w[l], a_log[l], dt_bias[l],
                                                 norm_a[l], sink[l], q_norm[l], k_norm[l], w_branch[l], w_out[l])
        xs = _latent_layer(xs, mod_lat, cos, sin, state_delta[:, l], cache_win_k[:, l], cache_win_v[:, l],
                           cache_glob_k[:, l], cache_glob_v[:, l], norm_g[l], w_in[l], conv_w[l], a_log[l],
                           dt_bias[l], norm_a[l], sink[l], q_norm[l], k_norm[l], w_branch[l], w_out[l])
        st_l.append(st)
        wk_l.append(kb)
        wv_l.append(vb)
        gk_l.append(kc)
        gv_l.append(vc)
    y_prompt = _rmsnorm(xp, final_g)
    y_sample = _rmsnorm(xs, final_g)
    new_state_delta = jnp.stack(st_l, axis=1)
    new_win_k = jnp.stack(wk_l, axis=1)
    new_win_v = jnp.stack(wv_l, axis=1)
    new_glob_k = jnp.stack(gk_l, axis=1)
    new_glob_v = jnp.stack(gv_l, axis=1)
    return (y_prompt, y_sample, new_state_delta, new_win_k, new_win_v, new_glob_k, new_glob_v)
```

```python
import functools
import math

import jax
import jax.numpy as jnp
import numpy as np
from jax import lax
from jax.experimental import pallas as pl
from jax.experimental.pallas import tpu as pltpu

HEAD_DIM = 128
KV_HEADS = 4
KV_W = KV_HEADS * HEAD_DIM
CHUNK = 64
WINDOW = 128
GRID_W = 64
CONV_K = 3
N_BRANCH = 3
ROPE_THETA = 10000.0
EPS = 1e-6
NEG = -1e30

LANES = 128
VMEM_LIMIT_BYTES = 56 * 1024 * 1024

F32 = jnp.float32
BF16 = jnp.bfloat16


def _params(*sem):
    return pltpu.CompilerParams(dimension_semantics=sem, vmem_limit_bytes=VMEM_LIMIT_BYTES)


def _dot(a, b):
    return jnp.dot(a, b, preferred_element_type=F32)


def _dot_nt(a, b):
    return lax.dot_general(a, b, (((1,), (1,)), ((), ())), preferred_element_type=F32)


def _dot_tn(a, b):
    return lax.dot_general(a, b, (((0,), (0,)), ((), ())), preferred_element_type=F32)


def _silu(x):
    return x * jax.nn.sigmoid(x)


def _with_alias(in_specs, args, out):
    if out is None:
        return in_specs, args, {}
    return in_specs + [pl.BlockSpec(memory_space=pl.ANY)], args + [out], {len(args): 0}


def _pick_tile(n, candidates):
    for c in candidates:
        if n % c == 0:
            return c
    raise ValueError(f"no tile in {candidates} divides {n}")


def _mod_kernel(c_ref, w_ref, b_ref, o_ref):
    a = _silu(c_ref[...]).astype(BF16)
    o_ref[...] = _dot(a, w_ref[...].astype(BF16)) + b_ref[...]


def _modulation(c_all, w_mod, b_mod):
    n_layers, d, n3 = w_mod.shape
    rows = c_all.shape[0]
    tn = _pick_tile(n3, (512, 256, 128))
    return pl.pallas_call(
        _mod_kernel,
        grid=(n_layers, n3 // tn),
        in_specs=[
            pl.BlockSpec((rows, d), lambda l, j: (0, 0)),
            pl.BlockSpec((None, d, tn), lambda l, j: (l, 0, j)),
            pl.BlockSpec((None, 1, tn), lambda l, j: (l, 0, j)),
        ],
        out_specs=pl.BlockSpec((None, rows, tn), lambda l, j: (l, 0, j)),
        out_shape=jax.ShapeDtypeStruct((n_layers, rows, n3), F32),
        compiler_params=_params("parallel", "parallel"),
        name="modulation",
    )(c_all, w_mod, b_mod.reshape(n_layers, 1, n3))


def _prep_kernel(x_ref, mod_ref, g_ref, o_ref):
    x = x_ref[...]
    y = x * lax.rsqrt(jnp.mean(x * x, axis=-1, keepdims=True) + EPS) * g_ref[...]
    shift = mod_ref[0:1, :]
    scale = mod_ref[1:2, :]
    o_ref[...] = (y * (1.0 + scale) + shift).astype(o_ref.dtype)


def _mod_row(i, tile, m_ctx, t_lat, ctx_row):
    start = i * tile
    return jnp.where(start < m_ctx, ctx_row, (start - m_ctx) // t_lat)


def _prep(x, mod_l, norm_g_l, m_ctx, t_lat, ctx_row, tile):
    m, d = x.shape
    return pl.pallas_call(
        _prep_kernel,
        grid=(m // tile,),
        in_specs=[
            pl.BlockSpec((tile, d), lambda i: (i, 0)),
            pl.BlockSpec((None, 3, d), lambda i: (_mod_row(i, tile, m_ctx, t_lat, ctx_row), 0, 0)),
            pl.BlockSpec((1, d), lambda i: (0, 0)),
        ],
        out_specs=pl.BlockSpec((tile, d), lambda i: (i, 0)),
        out_shape=jax.ShapeDtypeStruct((m, d), BF16),
        compiler_params=_params("parallel"),
        name="prep",
    )(x, mod_l, norm_g_l.reshape(1, d))


def _mm_kernel(a_ref, w_ref, o_ref):
    o_ref[...] = _dot(a_ref[...], w_ref[...]).astype(o_ref.dtype)


def _matmul(a, w, tm, tn, out_dtype, name):
    m, k = a.shape
    _, n = w.shape
    return pl.pallas_call(
        _mm_kernel,
        grid=(m // tm, n // tn),
        in_specs=[
            pl.BlockSpec((tm, k), lambda i, j: (i, 0)),
            pl.BlockSpec((k, tn), lambda i, j: (0, j)),
        ],
        out_specs=pl.BlockSpec((tm, tn), lambda i, j: (i, j)),
        out_shape=jax.ShapeDtypeStruct((m, n), out_dtype),
        compiler_params=_params("parallel", "arbitrary"),
        name=name,
    )(a, w)


def _conv_kernel(x_ref, w_ref, *rest, n_qk_blocks):
    o_ref = rest[-1]
    x = x_ref[...]
    t = x.shape[0]
    row = lax.broadcasted_iota(jnp.int32, x.shape, 0)
    prev = jnp.where(row == 0, 0.0, pltpu.roll(x, 1, 0))
    nxt = jnp.where(row == t - 1, 0.0, pltpu.roll(x, t - 1, 0))
    y = _silu(prev * w_ref[0:1, :] + x * w_ref[1:2, :] + nxt * w_ref[2:3, :])
    inv = lax.rsqrt(jnp.sum(y * y, axis=-1, keepdims=True) + EPS)
    is_qk = pl.program_id(1) < n_qk_blocks
    o_ref[...] = y * jnp.where(is_qk, inv, 1.0)


def _delta_conv(p, conv_w_l, seq_len, n_seq, row_block0, out, bw):
    n_col = 3 * bw // LANES
    kern = functools.partial(_conv_kernel, n_qk_blocks=2 * bw // LANES)
    m = p.shape[0]
    in_specs = [
        pl.BlockSpec((seq_len, LANES), lambda b, j: (row_block0 + b, j)),
        pl.BlockSpec((CONV_K, LANES), lambda b, j: (0, j)),
    ]
    args = [p, conv_w_l]
    in_specs, args, aliases = _with_alias(in_specs, args, out)
    return pl.pallas_call(
        kern,
        grid=(n_seq, n_col),
        in_specs=in_specs,
        out_specs=pl.BlockSpec((seq_len, LANES), lambda b, j: (row_block0 + b, j)),
        out_shape=jax.ShapeDtypeStruct((m, 3 * bw), F32),
        input_output_aliases=aliases,
        compiler_params=_params("parallel", "parallel"),
        name="delta_conv",
    )(*args)


def _gates_kernel(ba_ref, alog_ref, dtb_ref, o_ref, *, heads):
    ba = ba_ref[...]
    rows = ba.shape[0]
    lane = lax.broadcasted_iota(jnp.int32, ba.shape, 1)
    pos = lax.broadcasted_iota(jnp.int32, ba.shape, 0) & (CHUNK - 1)
    beta = jax.nn.sigmoid(ba)
    z = ba + dtb_ref[...]
    softplus = jnp.maximum(z, 0.0) + jnp.log1p(jnp.exp(-jnp.abs(z)))
    g = -jnp.exp(alog_ref[...]) * softplus
    pre = g
    suf = g
    step = 1
    while step < CHUNK:
        pre = pre + jnp.where(pos >= step, pltpu.roll(pre, step, 0), 0.0)
        suf = suf + jnp.where(pos < CHUNK - step, pltpu.roll(suf, rows - step, 0), 0.0)
        step *= 2
    gc = jnp.where(lane < 3 * heads, pre, suf)
    o_ref[...] = jnp.where(lane < 2 * heads, beta, jnp.where(lane < 4 * heads, gc, 0.0))


def _delta_gates(ba, a_log_l, dt_bias_l, heads, tile):
    m = ba.shape[0]
    pad = jnp.zeros((LANES - 4 * heads,), F32)
    lead = jnp.zeros((2 * heads,), F32)
    alog = jnp.concatenate([lead, a_log_l.reshape(-1), pad]).reshape(1, LANES)
    dtb = jnp.concatenate([lead, dt_bias_l.reshape(-1), pad]).reshape(1, LANES)
    return pl.pallas_call(
        functools.partial(_gates_kernel, heads=heads),
        grid=(m // tile,),
        in_specs=[
            pl.BlockSpec((tile, LANES), lambda i: (i, 0)),
            pl.BlockSpec((1, LANES), lambda i: (0, 0)),
            pl.BlockSpec((1, LANES), lambda i: (0, 0)),
        ],
        out_specs=pl.BlockSpec((tile, LANES), lambda i: (i, 0)),
        out_shape=jax.ShapeDtypeStruct((m, LANES), F32),
        compiler_params=_params("parallel"),
        name="delta_gates",
    )(ba, alog, dtb)


def _split_bf16(x):
    hi = x.astype(BF16)
    lo = (x - hi.astype(F32)).astype(BF16)
    return hi, lo


def _unit_triangular_inverse(a):
    n = a.shape[0]
    eye = (lax.broadcasted_iota(jnp.int32, (n, n), 0) == lax.broadcasted_iota(jnp.int32, (n, n), 1)).astype(F32)
    t = eye
    p = -a
    span = 1
    while span < n:
        x = jnp.concatenate([t, p], axis=0) if 2 * span < n else t
        x_hi, x_lo = _split_bf16(x)
        p_hi, p_lo = _split_bf16(p)
        y = _dot(x_hi, p_hi) + _dot(x_hi, p_lo) + _dot(x_lo, p_hi)
        t = t + y[:n]
        if 2 * span < n:
            p = y[n:]
        span *= 2
    return t


def _delta_kernel(*refs, heads, reverse, has_s0, finalize, emit_state, has_alias):
    it = iter(refs)
    q_ref, k_ref, v_ref, gb_ref, gbt_ref = (next(it) for _ in range(5))
    s0_ref = next(it) if has_s0 else None
    if finalize:
        oprev_ref, z_ref, na_ref = (next(it) for _ in range(3))
    if has_alias:
        next(it)
    o_ref = next(it)
    sout_ref = next(it) if emit_state else None
    s_ref = next(it)

    n = pl.program_id(1)
    n_chunks = q_ref.shape[0] // CHUNK
    direction = 1 if reverse else 0

    @pl.when(n == 0)
    def _():
        if has_s0:
            s_ref[...] = s0_ref[...]
        else:
            s_ref[...] = jnp.zeros_like(s_ref)

    ii = lax.broadcasted_iota(jnp.int32, (CHUNK, CHUNK), 0)
    jj = lax.broadcasted_iota(jnp.int32, (CHUNK, CHUNK), 1)
    if reverse:
        causal, strict = ii <= jj, ii < jj
    else:
        causal, strict = ii >= jj, ii > jj
    scale = HEAD_DIM ** -0.5

    def chunk_body(ci, carry):
        c = (n_chunks - 1 - ci) if reverse else ci
        r0 = pl.multiple_of(c * CHUNK, CHUNK)
        rows = pl.ds(r0, CHUNK)
        gb = gb_ref[rows, :]
        gbt = gbt_ref[c]
        for h in range(heads):
            cols = slice(h * HEAD_DIM, (h + 1) * HEAD_DIM)
            lb = direction * heads + h
            lg = 2 * heads + lb
            q = q_ref[rows, cols]
            k = k_ref[rows, cols]
            v = v_ref[rows, cols]
            beta = gb[:, lb:lb + 1]
            gc = gb[:, lg:lg + 1]
            grow = gbt[lg:lg + 1, :]
            glast = gc[0:1, :] if reverse else gc[CHUNK - 1:CHUNK, :]
            eg = jnp.exp(gc)
            decay = jnp.where(causal, jnp.exp(jnp.minimum(gc - grow, 0.0)), 0.0)
            kb = k * beta
            k16 = k.astype(BF16)
            a = jnp.where(strict, _dot_nt(kb.astype(BF16), k16) * decay, 0.0)
            t16 = _unit_triangular_inverse(a).astype(BF16)
            u = _dot(t16, (v * beta).astype(BF16))
            w = _dot(t16, (kb * eg).astype(BF16))
            qs = q * scale
            qk = _dot_nt(qs.astype(BF16), k16) * decay
            s = s_ref[h]
            s16 = s.astype(BF16)
            v_new = u - _dot(w.astype(BF16), s16)
            v16 = v_new.astype(BF16)
            o = _dot((qs * eg).astype(BF16), s16) + _dot(qk.astype(BF16), v16)
            kd = k * jnp.exp(glast - gc)
            s_ref[h] = s * jnp.exp(glast) + _dot_tn(kd.astype(BF16), v16)
            if finalize:
                o = o + oprev_ref[rows, cols]
                y = o * lax.rsqrt(jnp.mean(o * o, axis=-1, keepdims=True) + EPS) * na_ref[...]
                o_ref[rows, cols] = (y * _silu(z_ref[rows, cols])).astype(o_ref.dtype)
            else:
                o_ref[rows, cols] = o
        return carry

    lax.fori_loop(0, n_chunks, chunk_body, 0)

    if emit_state:
        @pl.when(n == pl.num_programs(1) - 1)
        def _():
            sout_ref[...] = s_ref[...]


def _delta_scan(qkv, gb, gbt, *, heads, seq_len, n_seq, row0, tb, reverse, s0=None, s0_index=None,
                o_prev=None, p=None, z_block=None, norm_a_l=None, out=None, emit_state=False):
    bw = heads * HEAD_DIM
    m = qkv.shape[0]
    ntb = seq_len // tb
    blk0 = row0 // tb
    finalize = o_prev is not None

    def tok(b, n):
        step = (ntb - 1 - n) if reverse else n
        return blk0 + b * ntb + step

    in_specs = [
        pl.BlockSpec((tb, bw), lambda b, n: (tok(b, n), 0)),
        pl.BlockSpec((tb, bw), lambda b, n: (tok(b, n), 1)),
        pl.BlockSpec((tb, bw), lambda b, n: (tok(b, n), 2)),
        pl.BlockSpec((tb, LANES), lambda b, n: (tok(b, n), 0)),
        pl.BlockSpec((tb // CHUNK, LANES, CHUNK), lambda b, n: (tok(b, n), 0, 0)),
    ]
    args = [qkv, qkv, qkv, gb, gbt]
    if s0 is not None:
        layer, direction = s0_index
        in_specs.append(pl.BlockSpec((None, None, None, heads, HEAD_DIM, HEAD_DIM),
                                     lambda b, n: (b, layer, direction, 0, 0, 0)))
        args.append(s0)
    if finalize:
        in_specs += [
            pl.BlockSpec((tb, bw), lambda b, n: (tok(b, n), 0)),
            pl.BlockSpec((tb, bw), lambda b, n: (tok(b, n), z_block)),
            pl.BlockSpec((1, HEAD_DIM), lambda b, n: (0, 0)),
        ]
        args += [o_prev, p, norm_a_l.reshape(1, HEAD_DIM)]
    in_specs, args, aliases = _with_alias(in_specs, args, out)
    out_specs = [pl.BlockSpec((tb, bw), lambda b, n: (tok(b, n), 0))]
    out_shape = [jax.ShapeDtypeStruct((m, bw), BF16 if finalize else F32)]
    if emit_state:
        out_specs.append(pl.BlockSpec((None, heads, HEAD_DIM, HEAD_DIM), lambda b, n: (b, 0, 0, 0)))
        out_shape.append(jax.ShapeDtypeStruct((n_seq, heads, HEAD_DIM, HEAD_DIM), F32))
    kern = functools.partial(_delta_kernel, heads=heads, reverse=reverse, has_s0=s0 is not None,
                             finalize=finalize, emit_state=emit_state, has_alias=out is not None)
    res = pl.pallas_call(
        kern,
        grid=(n_seq, ntb),
        in_specs=in_specs,
        out_specs=out_specs,
        out_shape=out_shape,
        scratch_shapes=[pltpu.VMEM((heads, HEAD_DIM, HEAD_DIM), F32)],
        input_output_aliases=aliases,
        compiler_params=_params("parallel", "arbitrary"),
        name="delta_scan_bwd" if reverse else "delta_scan_fwd",
    )(*args)
    return res if emit_state else (res[0], None)


def _rope(x, cosf, sins):
    return x * cosf + pltpu.roll(x, HEAD_DIM // 2, 1) * sins


def _rms_head(x, g):
    return x * lax.rsqrt(jnp.mean(x * x, axis=-1, keepdims=True) + EPS) * g


def _attn_kernel(*refs, band, has_ctx, use_sink, use_norm, use_rope, emit_k, has_alias, t_self, s_ctx, tq, grp,
                 key_chunks):
    it = iter(refs)
    sink_ref = next(it) if use_sink else None
    q_ref, k_ref, v_ref, z_ref = (next(it) for _ in range(4))
    if has_ctx:
        kc_ref, vc_ref = next(it), next(it)
    if use_norm:
        qn_ref, kn_ref = next(it), next(it)
    if use_rope:
        cq_ref, sq_ref, ck_ref, sk_ref = (next(it) for _ in range(4))
    if has_alias:
        next(it)
    y_ref = next(it)
    kout_ref = next(it) if emit_k else None
    ks_ref, vs_ref = next(it), next(it)

    h = pl.program_id(1)
    i = pl.program_id(2)
    pad = WINDOW if band else 0
    ctx0 = t_self + 2 * pad

    @pl.when(i == 0)
    def _():
        k = k_ref[...]
        if use_norm:
            k = _rms_head(k, kn_ref[...])
        if emit_k:
            kout_ref[...] = k
        if use_rope:
            k = _rope(k, ck_ref[...], sk_ref[...])
        ks_ref[pad:pad + t_self, :] = k.astype(BF16)
        vs_ref[pad:pad + t_self, :] = v_ref[...].astype(BF16)
        if band:
            zeros = jnp.zeros((pad, HEAD_DIM), BF16)
            ks_ref[0:pad, :] = zeros
            vs_ref[0:pad, :] = zeros
            ks_ref[pad + t_self:ctx0, :] = zeros
            vs_ref[pad + t_self:ctx0, :] = zeros
        if has_ctx:
            ks_ref[ctx0:ctx0 + s_ctx, :] = kc_ref[...].astype(BF16)
            vs_ref[ctx0:ctx0 + s_ctx, :] = vc_ref[...].astype(BF16)

    scale = HEAD_DIM ** -0.5
    qs = []
    for g in range(grp):
        q = q_ref[:, g * HEAD_DIM:(g + 1) * HEAD_DIM]
        if use_norm:
            q = _rms_head(q, qn_ref[...])
        if use_rope:
            q = _rope(q, cq_ref[...], sq_ref[...])
        qs.append((q * scale).astype(BF16))
    q16 = jnp.concatenate(qs, axis=0)
    rows = grp * tq

    if band:
        r0 = pl.multiple_of(i * tq, tq)
        width = 3 * WINDOW
        s_loc = _dot_nt(q16, ks_ref[pl.ds(r0, width), :])
        qpos = lax.broadcasted_iota(jnp.int32, (rows, width), 0) & (tq - 1)
        krel = lax.broadcasted_iota(jnp.int32, (rows, width), 1) - WINDOW
        kpos = krel + i * tq
        valid = (jnp.abs(qpos - krel) <= WINDOW) & (kpos >= 0) & (kpos < t_self)
        pieces = [(jnp.where(valid, s_loc, NEG), vs_ref[pl.ds(r0, width), :])]
        if has_ctx:
            pieces.append((_dot_nt(q16, ks_ref[ctx0:ctx0 + s_ctx, :]), vs_ref[ctx0:ctx0 + s_ctx, :]))
    else:
        total = t_self + (s_ctx if has_ctx else 0)
        kc = total // key_chunks
        pieces = [(_dot_nt(q16, ks_ref[c * kc:(c + 1) * kc, :]), vs_ref[c * kc:(c + 1) * kc, :])
                  for c in range(key_chunks)]

    m = functools.reduce(jnp.maximum, [jnp.max(s, axis=-1, keepdims=True) for s, _ in pieces])
    if use_sink:
        rid = lax.broadcasted_iota(jnp.int32, (rows, 1), 0)
        sink = jnp.zeros((rows, 1), F32)
        for g in range(grp):
            sink = jnp.where(rid >= g * tq, sink_ref[h * grp + g], sink)
        m = jnp.maximum(m, sink)
        denom = jnp.exp(sink - m)
    else:
        denom = jnp.zeros((rows, 1), F32)
    acc = jnp.zeros((rows, HEAD_DIM), F32)
    for s, v16 in pieces:
        p = jnp.exp(s - m)
        denom = denom + jnp.sum(p, axis=-1, keepdims=True)
        acc = acc + _dot(p.astype(BF16), v16)
    o = acc / denom
    for g in range(grp):
        cols = slice(g * HEAD_DIM, (g + 1) * HEAD_DIM)
        y_ref[:, cols] = (o[g * tq:(g + 1) * tq] * _silu(z_ref[:, cols])).astype(y_ref.dtype)


def _attention(p, *, q_block, k_block, v_block, z_block, seq_len, n_seq, row0, tq, heads, band, out,
               ctx_k=None, ctx_v=None, layer=None, sink_l=None, q_norm_l=None, k_norm_l=None, rope=None,
               emit_k=False):
    m = p.shape[0]
    bw = heads * HEAD_DIM
    grp = heads // KV_HEADS
    gw = grp * HEAD_DIM
    nq = seq_len // tq
    qblk0 = row0 // tq
    sblk0 = row0 // seq_len
    has_ctx = ctx_k is not None
    s_ctx = ctx_k.shape[2] if has_ctx else 0
    use_sink = sink_l is not None
    use_norm = q_norm_l is not None
    use_rope = rope is not None
    pad = WINDOW if band else 0
    total = seq_len + s_ctx
    key_chunks = 1
    while (total // key_chunks > 2304 or total % (key_chunks * LANES)) and key_chunks < total // LANES:
        key_chunks += 1

    in_specs, args = [], []
    if use_sink:
        in_specs.append(pl.BlockSpec(memory_space=pltpu.SMEM))
        args.append(sink_l)
    in_specs += [
        pl.BlockSpec((tq, gw), lambda b, h, i: (qblk0 + b * nq + i, q_block + h)),
        pl.BlockSpec((seq_len, HEAD_DIM), lambda b, h, i: (sblk0 + b, k_block + h)),
        pl.BlockSpec((seq_len, HEAD_DIM), lambda b, h, i: (sblk0 + b, v_block + h)),
        pl.BlockSpec((tq, gw), lambda b, h, i: (qblk0 + b * nq + i, z_block + h)),
    ]
    args += [p, p, p, p]
    if has_ctx:
        spec = pl.BlockSpec((None, None, s_ctx, HEAD_DIM), lambda b, h, i: (b, layer, 0, h))
        in_specs += [spec, spec]
        args += [ctx_k, ctx_v]
    if use_norm:
        spec = pl.BlockSpec((1, HEAD_DIM), lambda b, h, i: (0, 0))
        in_specs += [spec, spec]
        args += [q_norm_l.reshape(1, HEAD_DIM), k_norm_l.reshape(1, HEAD_DIM)]
    if use_rope:
        cosf, sins = rope
        qspec = pl.BlockSpec((tq, HEAD_DIM), lambda b, h, i: (i, 0))
        kspec = pl.BlockSpec((seq_len, HEAD_DIM), lambda b, h, i: (0, 0))
        in_specs += [qspec, qspec, kspec, kspec]
        args += [cosf, sins, cosf, sins]
    in_specs, args, aliases = _with_alias(in_specs, args, out)
    out_specs = [pl.BlockSpec((tq, gw), lambda b, h, i: (qblk0 + b * nq + i, h))]
    out_shape = [jax.ShapeDtypeStruct((m, bw), BF16)]
    if emit_k:
        out_specs.append(pl.BlockSpec((None, seq_len, HEAD_DIM), lambda b, h, i: (b, 0, h)))
        out_shape.append(jax.ShapeDtypeStruct((n_seq, seq_len, KV_W), F32))
    kern = functools.partial(_attn_kernel, band=band, has_ctx=has_ctx, use_sink=use_sink, use_norm=use_norm,
                             use_rope=use_rope, emit_k=emit_k, has_alias=out is not None, t_self=seq_len,
                             s_ctx=s_ctx, tq=tq, grp=grp, key_chunks=key_chunks)
    n_keys = seq_len + 2 * pad + s_ctx
    assert tq & (tq - 1) == 0 and (not band or tq == WINDOW)
    res = pl.pallas_call(
        kern,
        grid=(n_seq, KV_HEADS, nq),
        in_specs=in_specs,
        out_specs=out_specs,
        out_shape=out_shape,
        scratch_shapes=[pltpu.VMEM((n_keys, HEAD_DIM), BF16), pltpu.VMEM((n_keys, HEAD_DIM), BF16)],
        input_output_aliases=aliases,
        compiler_params=_params("parallel", "parallel", "arbitrary"),
        name=("attn_band" if band else "attn_full") + ("_ctx" if has_ctx else ""),
    )(*args)
    return res if emit_k else (res[0], None)


def _merge_kernel(ya_ref, yb_ref, yc_ref, w_ref, ga_ref, gb_ref, gc_ref, o_ref):
    acc = jax.nn.sigmoid(ga_ref[...]) * _dot(ya_ref[...], w_ref[0])
    acc = acc + jax.nn.sigmoid(gb_ref[...]) * _dot(yb_ref[...], w_ref[1])
    acc = acc + jax.nn.sigmoid(gc_ref[...]) * _dot(yc_ref[...], w_ref[2])
    o_ref[...] = acc.astype(o_ref.dtype)


def _merge(ya, yb, yc, w_branch_l, p, gate_col0, tm, tn):
    m, bw = ya.shape
    d = w_branch_l.shape[2]
    g0 = gate_col0 // tn
    gstep = d // tn
    br = pl.BlockSpec((tm, bw), lambda i, j: (i, 0))
    return pl.pallas_call(
        _merge_kernel,
        grid=(m // tm, d // tn),
        in_specs=[
            br, br, br,
            pl.BlockSpec((N_BRANCH, bw, tn), lambda i, j: (0, 0, j)),
            pl.BlockSpec((tm, tn), lambda i, j: (i, g0 + j)),
            pl.BlockSpec((tm, tn), lambda i, j: (i, g0 + gstep + j)),
            pl.BlockSpec((tm, tn), lambda i, j: (i, g0 + 2 * gstep + j)),
        ],
        out_specs=pl.BlockSpec((tm, tn), lambda i, j: (i, j)),
        out_shape=jax.ShapeDtypeStruct((m, d), BF16),
        compiler_params=_params("parallel", "arbitrary"),
        name="merge",
    )(ya, yb, yc, w_branch_l, p, p, p)


def _out_kernel(a_ref, w_ref, x_ref, mod_ref, o_ref):
    o_ref[...] = x_ref[...] + mod_ref[2:3, :] * _dot(a_ref[...], w_ref[...])


def _out_proj(mixed, w_out_l, x, mod_l, m_ctx, t_lat, ctx_row, tm, tn):
    m, d = x.shape
    return pl.pallas_call(
        _out_kernel,
        grid=(m // tm, d // tn),
        in_specs=[
            pl.BlockSpec((tm, d), lambda i, j: (i, 0)),
            pl.BlockSpec((d, tn), lambda i, j: (0, j)),
            pl.BlockSpec((tm, tn), lambda i, j: (i, j)),
            pl.BlockSpec((None, 3, tn), lambda i, j: (_mod_row(i, tm, m_ctx, t_lat, ctx_row), 0, j)),
        ],
        out_specs=pl.BlockSpec((tm, tn), lambda i, j: (i, j)),
        out_shape=jax.ShapeDtypeStruct((m, d), F32),
        compiler_params=_params("parallel", "arbitrary"),
        name="out_proj",
    )(mixed, w_out_l, x, mod_l)


def _final_kernel(x_ref, g_ref, o_ref):
    x = x_ref[...]
    o_ref[...] = x * lax.rsqrt(jnp.mean(x * x, axis=-1, keepdims=True) + EPS) * g_ref[...]


def _final_norm(x, g, row0, n_rows, tile):
    d = x.shape[1]
    blk0 = row0 // tile
    return pl.pallas_call(
        _final_kernel,
        grid=(n_rows // tile,),
        in_specs=[pl.BlockSpec((tile, d), lambda i: (blk0 + i, 0)), pl.BlockSpec((1, d), lambda i: (0, 0))],
        out_specs=pl.BlockSpec((tile, d), lambda i: (i, 0)),
        out_shape=jax.ShapeDtypeStruct((n_rows, d), F32),
        compiler_params=_params("parallel"),
        name="final_norm",
    )(x, g.reshape(1, d))


def _axial_rope_tables(n_tokens):
    rows = n_tokens // GRID_W
    row = jnp.repeat(jnp.arange(rows), GRID_W).astype(F32)
    col = jnp.tile(jnp.arange(GRID_W), rows).astype(F32)
    n_freq = HEAD_DIM // 4
    inv = ROPE_THETA ** (-jnp.arange(n_freq, dtype=F32) / n_freq)
    ang = jnp.concatenate([row[:, None] * inv, col[:, None] * inv], axis=-1)
    cos, sin = jnp.cos(ang), jnp.sin(ang)
    return jnp.concatenate([cos, cos], axis=-1), jnp.concatenate([-sin, sin], axis=-1)


def kernel(x_prompt, x_sample, state_delta, cache_win_k, cache_win_v, cache_glob_k, cache_glob_v, c, c_ctx,
           norm_g, w_mod, b_mod, w_in, conv_w, a_log, dt_bias, norm_a, sink, q_norm, k_norm, w_branch, w_out,
           final_g):
    n_layers, d = norm_g.shape
    bw = d // 2
    heads = bw // HEAD_DIM
    nbc, s_len, _ = x_prompt.shape
    nbl, t_len, _ = x_sample.shape
    past = cache_win_k.shape[2]
    m_ctx, m_lat = nbc * s_len, nbl * t_len
    m = m_ctx + m_lat
    assert m_ctx % t_len == 0 and t_len % s_len == 0 and s_len % (2 * WINDOW) == 0 and 4 * heads <= LANES
    assert heads % KV_HEADS == 0 and bw % KV_W == 0

    widths = dict(a_qkv=3 * bw, a_z=bw, b_q=bw, b_k=KV_W, b_v=KV_W, b_z=bw, c_q=bw, c_k=KV_W, c_v=KV_W, c_z=bw,
                  gates=N_BRANCH * d)
    off, col = {}, 0
    for name, wd in widths.items():
        off[name] = col
        col += wd
    n_main = col
    n_ba = 4 * heads
    src = 4 * bw

    tile_seq = s_len
    tm_big = _pick_tile(math.gcd(m_ctx, t_len), (1024, 512, 256))
    tm_merge = _pick_tile(math.gcd(m_ctx, t_len), (512, 256))
    tn = _pick_tile(math.gcd(d, KV_W), (512,))
    tb = 4 * CHUNK

    w_main = jnp.concatenate([w_in[:, :, :src], w_in[:, :, src + n_ba:]], axis=-1).astype(BF16)
    w_ba = jnp.pad(w_in[:, :, src:src + n_ba], ((0, 0), (0, 0), (0, LANES - n_ba))).astype(BF16)
    w_branch16 = w_branch.astype(BF16)
    w_out16 = w_out.astype(BF16)

    ctx_row = nbl
    mod_rows = -(-(nbl + 1) // 8) * 8
    c_all = jnp.concatenate([c, c_ctx[None], jnp.zeros((mod_rows - nbl - 1, d), F32)], axis=0)
    mod = _modulation(c_all, w_mod, b_mod).reshape(n_layers, mod_rows, 3, d)

    rope = _axial_rope_tables(t_len)
    ckw = cache_win_k.reshape(nbl, n_layers, past, KV_W)
    cvw = cache_win_v.reshape(nbl, n_layers, past, KV_W)
    ckg = cache_glob_k.reshape(nbl, n_layers, past, KV_W)
    cvg = cache_glob_v.reshape(nbl, n_layers, past, KV_W)

    x = jnp.concatenate([x_prompt.reshape(m_ctx, d), x_sample.reshape(m_lat, d)], axis=0)
    states, win_k, win_v, glob_k, glob_v = [], [], [], [], []
    for l in range(n_layers):
        h = _prep(x, mod[l], norm_g[l], m_ctx, t_len, ctx_row, tile_seq)
        p = _matmul(h, w_main[l], tm_big, tn, F32, "in_proj")
        ba = _matmul(h, w_ba[l], tm_big, LANES, F32, "in_proj_gates")

        qkv = _delta_conv(p, conv_w[l], s_len, nbc, 0, None, bw)
        qkv = _delta_conv(p, conv_w[l], t_len, nbl, m_ctx // t_len, qkv, bw)
        gb = _delta_gates(ba, a_log[l], dt_bias[l], heads, tile_seq)
        gbt = jnp.swapaxes(gb.reshape(m // CHUNK, CHUNK, LANES), 1, 2)
        common = dict(heads=heads, tb=tb)
        fin = dict(p=p, z_block=off["a_z"] // bw, norm_a_l=norm_a[l])
        o_f, s_f = _delta_scan(qkv, gb, gbt, seq_len=s_len, n_seq=nbc, row0=0, reverse=False,
                               emit_state=True, **common)
        o_f, _ = _delta_scan(qkv, gb, gbt, seq_len=t_len, n_seq=nbl, row0=m_ctx, reverse=False, out=o_f,
                             s0=state_delta, s0_index=(l, 0), **common)
        ya, s_b = _delta_scan(qkv, gb, gbt, seq_len=s_len, n_seq=nbc, row0=0, reverse=True,
                              o_prev=o_f, emit_state=True, **common, **fin)
        ya, _ = _delta_scan(qkv, gb, gbt, seq_len=t_len, n_seq=nbl, row0=m_ctx, reverse=True, out=ya,
                            s0=state_delta, s0_index=(l, 1), o_prev=o_f, **common, **fin)
        states.append(jnp.stack([s_f, s_b], axis=1))

        gw = (heads // KV_HEADS) * HEAD_DIM
        blocks_b = dict(q_block=off["b_q"] // gw, k_block=off["b_k"] // HEAD_DIM, v_block=off["b_v"] // HEAD_DIM,
                        z_block=off["b_z"] // gw)
        blocks_c = dict(q_block=off["c_q"] // gw, k_block=off["c_k"] // HEAD_DIM, v_block=off["c_v"] // HEAD_DIM,
                        z_block=off["c_z"] // gw)
        ctx_args = dict(seq_len=s_len, n_seq=nbc, row0=0, tq=s_len, heads=heads, band=False, out=None)
        lat_args = dict(seq_len=t_len, n_seq=nbl, row0=m_ctx, tq=WINDOW, heads=heads, layer=l, rope=rope)
        yb, _ = _attention(p, sink_l=sink[l], **blocks_b, **ctx_args)
        yb, _ = _attention(p, out=yb, sink_l=sink[l], band=True, ctx_k=ckw, ctx_v=cvw, **blocks_b, **lat_args)
        yc, kn = _attention(p, q_norm_l=q_norm[l], k_norm_l=k_norm[l], emit_k=True, **blocks_c, **ctx_args)
        yc, _ = _attention(p, out=yc, q_norm_l=q_norm[l], k_norm_l=k_norm[l], band=False, ctx_k=ckg, ctx_v=cvg,
                           **blocks_c, **lat_args)

        def ctx_slab(name):
            return p[:m_ctx, off[name]:off[name] + KV_W].reshape(nbc, s_len, KV_HEADS, HEAD_DIM)

        win_k.append(ctx_slab("b_k"))
        win_v.append(ctx_slab("b_v"))
        glob_k.append(kn.reshape(nbc, s_len, KV_HEADS, HEAD_DIM))
        glob_v.append(ctx_slab("c_v"))

        mixed = _merge(ya, yb, yc, w_branch16[l], p, off["gates"], tm_merge, tn)
        x = _out_proj(mixed, w_out16[l], x, mod[l], m_ctx, t_len, ctx_row, tm_big, tn)

    y_prompt = _final_norm(x, final_g, 0, m_ctx, tile_seq).reshape(nbc, s_len, d)
    y_sample = _final_norm(x, final_g, m_ctx, m_lat, tile_seq).reshape(nbl, t_len, d)
    return (y_prompt, y_sample, jnp.stack(states, axis=1), jnp.stack(win_k, axis=1), jnp.stack(win_v, axis=1),
            jnp.stack(glob_k, axis=1), jnp.stack(glob_v, axis=1))
```

```python
import functools
import math

import jax
import jax.numpy as jnp
import numpy as np
from jax import lax
from jax.experimental import pallas as pl
from jax.experimental.pallas import tpu as pltpu

HEAD_DIM = 128
KV_HEADS = 4
KV_W = KV_HEADS * HEAD_DIM
CHUNK = 64
WINDOW = 128
GRID_W = 64
CONV_K = 3
N_BRANCH = 3
ROPE_THETA = 10000.0
EPS = 1e-6
NEG = -1e30
LOG2E = math.log2(math.e)
KEY_CHUNK = 1024

LANES = 128
VMEM_LIMIT_BYTES = 56 * 1024 * 1024

F32 = jnp.float32
BF16 = jnp.bfloat16


def _params(*sem):
    return pltpu.CompilerParams(dimension_semantics=sem, vmem_limit_bytes=VMEM_LIMIT_BYTES)


def _dot(a, b):
    return jnp.dot(a, b, preferred_element_type=F32)


def _dot_nt(a, b):
    return lax.dot_general(a, b, (((1,), (1,)), ((), ())), preferred_element_type=F32)


def _dot_tn(a, b):
    return lax.dot_general(a, b, (((0,), (0,)), ((), ())), preferred_element_type=F32)


def _silu(x):
    return x * jax.nn.sigmoid(x)


def _with_alias(in_specs, args, out):
    if out is None:
        return in_specs, args, {}
    return in_specs + [pl.BlockSpec(memory_space=pl.ANY)], args + [out], {len(args): 0}


def _pick_tile(n, candidates):
    for c in candidates:
        if n % c == 0:
            return c
    raise ValueError(f"no tile in {candidates} divides {n}")


def _mod_kernel(c_ref, w_ref, b_ref, o_ref):
    a = _silu(c_ref[...]).astype(BF16)
    o_ref[...] = _dot(a, w_ref[...].astype(BF16)) + b_ref[...]


def _modulation(c_all, w_mod, b_mod):
    n_layers, d, n3 = w_mod.shape
    rows = c_all.shape[0]
    tn = _pick_tile(n3, (512, 256, 128))
    return pl.pallas_call(
        _mod_kernel,
        grid=(n_layers, n3 // tn),
        in_specs=[
            pl.BlockSpec((rows, d), lambda l, j: (0, 0)),
            pl.BlockSpec((None, d, tn), lambda l, j: (l, 0, j)),
            pl.BlockSpec((None, 1, tn), lambda l, j: (l, 0, j)),
        ],
        out_specs=pl.BlockSpec((None, rows, tn), lambda l, j: (l, 0, j)),
        out_shape=jax.ShapeDtypeStruct((n_layers, rows, n3), F32),
        compiler_params=_params("parallel", "parallel"),
        name="modulation",
    )(c_all, w_mod, b_mod.reshape(n_layers, 1, n3))


def _prep_kernel(x_ref, mod_ref, g_ref, o_ref):
    x = x_ref[...]
    y = x * lax.rsqrt(jnp.mean(x * x, axis=-1, keepdims=True) + EPS) * g_ref[...]
    shift = mod_ref[0:1, :]
    scale = mod_ref[1:2, :]
    o_ref[...] = (y * (1.0 + scale) + shift).astype(o_ref.dtype)


def _mod_row(i, tile, m_ctx, t_lat, ctx_row):
    start = i * tile
    return jnp.where(start < m_ctx, ctx_row, (start - m_ctx) // t_lat)


def _prep(x, mod_l, norm_g_l, m_ctx, t_lat, ctx_row, tile):
    m, d = x.shape
    return pl.pallas_call(
        _prep_kernel,
        grid=(m // tile,),
        in_specs=[
            pl.BlockSpec((tile, d), lambda i: (i, 0)),
            pl.BlockSpec((None, 3, d), lambda i: (_mod_row(i, tile, m_ctx, t_lat, ctx_row), 0, 0)),
            pl.BlockSpec((1, d), lambda i: (0, 0)),
        ],
        out_specs=pl.BlockSpec((tile, d), lambda i: (i, 0)),
        out_shape=jax.ShapeDtypeStruct((m, d), BF16),
        compiler_params=_params("parallel"),
        name="prep",
    )(x, mod_l, norm_g_l.reshape(1, d))


def _mm_kernel(a_ref, w_ref, o_ref):
    o_ref[...] = _dot(a_ref[...], w_ref[...]).astype(o_ref.dtype)


def _matmul(a, w, tm, tn, out_dtype, name):
    m, k = a.shape
    _, n = w.shape
    return pl.pallas_call(
        _mm_kernel,
        grid=(m // tm, n // tn),
        in_specs=[
            pl.BlockSpec((tm, k), lambda i, j: (i, 0)),
            pl.BlockSpec((k, tn), lambda i, j: (0, j)),
        ],
        out_specs=pl.BlockSpec((tm, tn), lambda i, j: (i, j)),
        out_shape=jax.ShapeDtypeStruct((m, n), out_dtype),
        compiler_params=_params("parallel", "arbitrary"),
        name=name,
    )(a, w)


def _conv_kernel(x_ref, w_ref, *rest, n_qk_blocks):
    o_ref = rest[-1]
    x = x_ref[...]
    t = x.shape[0]
    row = lax.broadcasted_iota(jnp.int32, x.shape, 0)
    prev = jnp.where(row == 0, 0.0, pltpu.roll(x, 1, 0))
    nxt = jnp.where(row == t - 1, 0.0, pltpu.roll(x, t - 1, 0))
    y = _silu(prev * w_ref[0:1, :] + x * w_ref[1:2, :] + nxt * w_ref[2:3, :])
    inv = lax.rsqrt(jnp.sum(y * y, axis=-1, keepdims=True) + EPS)
    is_qk = pl.program_id(1) < n_qk_blocks
    o_ref[...] = y * jnp.where(is_qk, inv, 1.0)


def _delta_conv(p, conv_w_l, seq_len, n_seq, row_block0, out, bw):
    n_col = 3 * bw // LANES
    kern = functools.partial(_conv_kernel, n_qk_blocks=2 * bw // LANES)
    m = p.shape[0]
    in_specs = [
        pl.BlockSpec((seq_len, LANES), lambda b, j: (row_block0 + b, j)),
        pl.BlockSpec((CONV_K, LANES), lambda b, j: (0, j)),
    ]
    args = [p, conv_w_l]
    in_specs, args, aliases = _with_alias(in_specs, args, out)
    return pl.pallas_call(
        kern,
        grid=(n_seq, n_col),
        in_specs=in_specs,
        out_specs=pl.BlockSpec((seq_len, LANES), lambda b, j: (row_block0 + b, j)),
        out_shape=jax.ShapeDtypeStruct((m, 3 * bw), F32),
        input_output_aliases=aliases,
        compiler_params=_params("parallel", "parallel"),
        name="delta_conv",
    )(*args)


def _gates_kernel(ba_ref, alog_ref, dtb_ref, o_ref, *, heads):
    ba = ba_ref[...]
    rows = ba.shape[0]
    lane = lax.broadcasted_iota(jnp.int32, ba.shape, 1)
    pos = lax.broadcasted_iota(jnp.int32, ba.shape, 0) & (CHUNK - 1)
    beta = jax.nn.sigmoid(ba)
    z = ba + dtb_ref[...]
    softplus = jnp.maximum(z, 0.0) + jnp.log1p(jnp.exp(-jnp.abs(z)))
    g = -jnp.exp(alog_ref[...]) * softplus
    pre = g
    suf = g
    step = 1
    while step < CHUNK:
        pre = pre + jnp.where(pos >= step, pltpu.roll(pre, step, 0), 0.0)
        suf = suf + jnp.where(pos < CHUNK - step, pltpu.roll(suf, rows - step, 0), 0.0)
        step *= 2
    gc = jnp.where(lane < 3 * heads, pre, suf)
    total = pre + suf - g
    eg = jnp.exp(pltpu.roll(gc, 2 * heads, 1))
    ekd = jnp.exp(pltpu.roll(total - gc, 4 * heads, 1))
    o_ref[...] = jnp.where(lane < 2 * heads, beta,
                           jnp.where(lane < 4 * heads, gc, jnp.where(lane < 6 * heads, eg, ekd)))


def _delta_gates(ba, a_log_l, dt_bias_l, heads, tile):
    m = ba.shape[0]
    pad = jnp.zeros((LANES - 4 * heads,), F32)
    lead = jnp.zeros((2 * heads,), F32)
    alog = jnp.concatenate([lead, a_log_l.reshape(-1), pad]).reshape(1, LANES)
    dtb = jnp.concatenate([lead, dt_bias_l.reshape(-1), pad]).reshape(1, LANES)
    return pl.pallas_call(
        functools.partial(_gates_kernel, heads=heads),
        grid=(m // tile,),
        in_specs=[
            pl.BlockSpec((tile, LANES), lambda i: (i, 0)),
            pl.BlockSpec((1, LANES), lambda i: (0, 0)),
            pl.BlockSpec((1, LANES), lambda i: (0, 0)),
        ],
        out_specs=pl.BlockSpec((tile, LANES), lambda i: (i, 0)),
        out_shape=jax.ShapeDtypeStruct((m, LANES), F32),
        compiler_params=_params("parallel"),
        name="delta_gates",
    )(ba, alog, dtb)


QUAD = 4
QW = QUAD * CHUNK
QD = QUAD * HEAD_DIM


def _split_bf16(x):
    hi = x.astype(BF16)
    lo = (x - hi.astype(F32)).astype(BF16)
    return hi, lo


def _tile_rows(x):
    return jnp.concatenate([x] * QUAD, axis=0)


def _quad_inverse_step(t, p, bd_mask, last):
    t_hi, t_lo = _split_bf16(t)
    p_hi, p_lo = _split_bf16(p)
    if last:
        x_hi, x_lo = t_hi, t_lo
    else:
        x_hi = jnp.concatenate([t_hi, p_hi], axis=0)
        x_lo = jnp.concatenate([t_lo, p_lo], axis=0)
    bd_hi = _tile_rows(p_hi) * bd_mask
    bd_lo = _tile_rows(p_lo) * bd_mask
    lhs = jnp.concatenate([x_hi, x_lo, x_hi], axis=1)
    rhs = jnp.concatenate([bd_hi, bd_hi, bd_lo], axis=0)
    y = _dot(lhs, rhs)
    return t + y[:CHUNK], (None if last else y[CHUNK:])


def _delta_kernel(*refs, heads, reverse, has_s0, finalize, emit_state, has_alias):
    it = iter(refs)
    q_ref, k_ref, v_ref, gb_ref, gq_ref = (next(it) for _ in range(5))
    s0_ref = next(it) if has_s0 else None
    if finalize:
        oprev_ref, z_ref, na_ref = (next(it) for _ in range(3))
    if has_alias:
        next(it)
    o_ref = next(it)
    sout_ref = next(it) if emit_state else None
    s_ref = next(it)

    n = pl.program_id(1)
    n_chunks = q_ref.shape[0] // CHUNK
    n_quads = heads // QUAD
    direction = 1 if reverse else 0
    quads = range(n_quads)

    @pl.when(n == 0)
    def _():
        if has_s0:
            s_ref[...] = s0_ref[...]
        else:
            s_ref[...] = jnp.zeros_like(s_ref)

    ii = lax.broadcasted_iota(jnp.int32, (CHUNK, QW), 0)
    jq = lax.broadcasted_iota(jnp.int32, (CHUNK, QW), 1)
    jj = jq & (CHUNK - 1)
    if reverse:
        causal, strict = ii <= jj, ii < jj
    else:
        causal, strict = ii >= jj, ii > jj
    eye = jnp.where(ii == jj, 1.0, 0.0)
    br = lax.broadcasted_iota(jnp.int32, (QW, QW), 0) // CHUNK
    bd_mask = jnp.where(br == lax.broadcasted_iota(jnp.int32, (QW, QW), 1) // CHUNK, 1.0, 0.0).astype(BF16)
    kr = lax.broadcasted_iota(jnp.int32, (QW, QD), 0) // CHUNK
    k_mask = jnp.where(kr == lax.broadcasted_iota(jnp.int32, (QW, QD), 1) // HEAD_DIM, 1.0, 0.0).astype(BF16)
    zeros_s = jnp.zeros((HEAD_DIM, HEAD_DIM), BF16)
    scale = HEAD_DIM ** -0.5

    def chunk_body(ci, carry):
        c = (n_chunks - 1 - ci) if reverse else ci
        r0 = pl.multiple_of(c * CHUNK, CHUNK)
        rows = pl.ds(r0, CHUNK)
        gb = gb_ref[rows, :]
        gq = gq_ref[c]
        edge = 0 if reverse else CHUNK - 1

        def per_head(group, qd):
            l0 = group * 2 * heads + direction * heads + qd * QUAD
            return jnp.concatenate(
                [jnp.broadcast_to(gb[:, l0 + j:l0 + j + 1], (CHUNK, HEAD_DIM)) for j in range(QUAD)], axis=1)

        def per_head_narrow(group, qd):
            l0 = group * 2 * heads + direction * heads + qd * QUAD
            out = jnp.broadcast_to(gb[:, l0 + QUAD - 1:l0 + QUAD], (CHUNK, QW))
            for j in range(QUAD - 2, -1, -1):
                out = jnp.where(jq < (j + 1) * CHUNK, jnp.broadcast_to(gb[:, l0 + j:l0 + j + 1], (CHUNK, QW)), out)
            return out

        cols = [slice(qd * QD, (qd + 1) * QD) for qd in quads]
        k = [k_ref[rows, cols[qd]] for qd in quads]
        beta = [per_head(0, qd) for qd in quads]
        eg = [per_head(2, qd) for qd in quads]
        kb = [k[qd] * beta[qd] for qd in quads]
        k16 = [k[qd].astype(BF16) for qd in quads]
        qs = [q_ref[rows, cols[qd]] * scale for qd in quads]

        kq = [_dot_nt(jnp.concatenate([kb[qd].astype(BF16), qs[qd].astype(BF16)], axis=0),
                      _tile_rows(k16[qd]) * k_mask) for qd in quads]
        decay = []
        for qd in quads:
            grow = gq[direction * n_quads + qd:direction * n_quads + qd + 1, :]
            decay.append(jnp.where(causal, jnp.exp(jnp.minimum(per_head_narrow(1, qd) - grow, 0.0)), 0.0))
        qk16 = [(kq[qd][CHUNK:] * decay[qd]).astype(BF16) for qd in quads]

        t = [eye for _ in quads]
        p = [-jnp.where(strict, kq[qd][:CHUNK] * decay[qd], 0.0) for qd in quads]
        n_steps = CHUNK.bit_length() - 1
        for step in range(n_steps):
            res = [_quad_inverse_step(t[qd], p[qd], bd_mask, step == n_steps - 1) for qd in quads]
            t = [r[0] for r in res]
            p = [r[1] for r in res]

        uw = []
        for qd in quads:
            vb16 = (v_ref[rows, cols[qd]] * beta[qd]).astype(BF16)
            kbe16 = (kb[qd] * eg[qd]).astype(BF16)
            rhs = jnp.concatenate(
                [jnp.concatenate([vb16[:, j * HEAD_DIM:(j + 1) * HEAD_DIM],
                                  kbe16[:, j * HEAD_DIM:(j + 1) * HEAD_DIM]], axis=1) for j in range(QUAD)], axis=0)
            uw.append(_dot(_tile_rows(t[qd].astype(BF16)) * bd_mask, rhs))

        s_old = [s_ref[h] for h in range(heads)]
        v_new = [None] * heads
        o_inter = [None] * heads
        for qd in quads:
            qd16 = (qs[qd] * eg[qd]).astype(BF16)
            for pr in range(QUAD // 2):
                j0, j1 = 2 * pr, 2 * pr + 1
                h0, h1 = qd * QUAD + j0, qd * QUAD + j1
                w16 = jnp.concatenate([uw[qd][j0 * CHUNK:(j0 + 1) * CHUNK, HEAD_DIM:],
                                       uw[qd][j1 * CHUNK:(j1 + 1) * CHUNK, HEAD_DIM:]], axis=1).astype(BF16)
                lhs = jnp.concatenate([w16, qd16[:, j0 * HEAD_DIM:(j1 + 1) * HEAD_DIM]], axis=0)
                bd_s = jnp.concatenate(
                    [jnp.concatenate([s_old[h0].astype(BF16), zeros_s], axis=1),
                     jnp.concatenate([zeros_s, s_old[h1].astype(BF16)], axis=1)], axis=0)
                ws = _dot(lhs, bd_s)
                for j, h in ((j0, h0), (j1, h1)):
                    lanes = slice((j - j0) * HEAD_DIM, (j - j0 + 1) * HEAD_DIM)
                    v_new[h] = uw[qd][j * CHUNK:(j + 1) * CHUNK, :HEAD_DIM] - ws[:CHUNK, lanes]
                    o_inter[h] = ws[CHUNK:, lanes]
        v16 = [v_new[h].astype(BF16) for h in range(heads)]
        o_intra = [_dot(_tile_rows(qk16[qd]) * bd_mask,
                        jnp.concatenate(v16[qd * QUAD:(qd + 1) * QUAD], axis=0)) for qd in quads]

        for qd in quads:
            kd16 = (k[qd] * per_head(3, qd)).astype(BF16)
            for j in range(QUAD):
                h = qd * QUAD + j
                hc = slice(h * HEAD_DIM, (h + 1) * HEAD_DIM)
                lg = 4 * heads + direction * heads + h
                g_tot = gb[edge:edge + 1, lg:lg + 1]
                s_ref[h] = s_old[h] * g_tot + _dot_tn(kd16[:, j * HEAD_DIM:(j + 1) * HEAD_DIM], v16[h])
                o = o_inter[h] + o_intra[qd][j * CHUNK:(j + 1) * CHUNK]
                if finalize:
                    o = o + oprev_ref[rows, hc]
                    y = o * lax.rsqrt(jnp.mean(o * o, axis=-1, keepdims=True) + EPS) * na_ref[...]
                    o_ref[rows, hc] = (y * _silu(z_ref[rows, hc])).astype(o_ref.dtype)
                else:
                    o_ref[rows, hc] = o
        return carry

    lax.fori_loop(0, n_chunks, chunk_body, 0)

    if emit_state:
        @pl.when(n == pl.num_programs(1) - 1)
        def _():
            sout_ref[...] = s_ref[...]


def _delta_scan(qkv, gb, gq, *, heads, seq_len, n_seq, row0, tb, reverse, s0=None, s0_index=None,
                o_prev=None, p=None, z_block=None, norm_a_l=None, out=None, emit_state=False):
    bw = heads * HEAD_DIM
    m = qkv.shape[0]
    ntb = seq_len // tb
    blk0 = row0 // tb
    finalize = o_prev is not None

    def tok(b, n):
        step = (ntb - 1 - n) if reverse else n
        return blk0 + b * ntb + step

    in_specs = [
        pl.BlockSpec((tb, bw), lambda b, n: (tok(b, n), 0)),
        pl.BlockSpec((tb, bw), lambda b, n: (tok(b, n), 1)),
        pl.BlockSpec((tb, bw), lambda b, n: (tok(b, n), 2)),
        pl.BlockSpec((tb, LANES), lambda b, n: (tok(b, n), 0)),
        pl.BlockSpec((tb // CHUNK, 2 * heads // QUAD, QW), lambda b, n: (tok(b, n), 0, 0)),
    ]
    args = [qkv, qkv, qkv, gb, gq]
    if s0 is not None:
        layer, direction = s0_index
        in_specs.append(pl.BlockSpec((None, None, None, heads, HEAD_DIM, HEAD_DIM),
                                     lambda b, n: (b, layer, direction, 0, 0, 0)))
        args.append(s0)
    if finalize:
        in_specs += [
            pl.BlockSpec((tb, bw), lambda b, n: (tok(b, n), 0)),
            pl.BlockSpec((tb, bw), lambda b, n: (tok(b, n), z_block)),
            pl.BlockSpec((1, HEAD_DIM), lambda b, n: (0, 0)),
        ]
        args += [o_prev, p, norm_a_l.reshape(1, HEAD_DIM)]
    in_specs, args, aliases = _with_alias(in_specs, args, out)
    out_specs = [pl.BlockSpec((tb, bw), lambda b, n: (tok(b, n), 0))]
    out_shape = [jax.ShapeDtypeStruct((m, bw), BF16 if finalize else F32)]
    if emit_state:
        out_specs.append(pl.BlockSpec((None, heads, HEAD_DIM, HEAD_DIM), lambda b, n: (b, 0, 0, 0)))
        out_shape.append(jax.ShapeDtypeStruct((n_seq, heads, HEAD_DIM, HEAD_DIM), F32))
    kern = functools.partial(_delta_kernel, heads=heads, reverse=reverse, has_s0=s0 is not None,
                             finalize=finalize, emit_state=emit_state, has_alias=out is not None)
    res = pl.pallas_call(
        kern,
        grid=(n_seq, ntb),
        in_specs=in_specs,
        out_specs=out_specs,
        out_shape=out_shape,
        scratch_shapes=[pltpu.VMEM((heads, HEAD_DIM, HEAD_DIM), F32)],
        input_output_aliases=aliases,
        compiler_params=_params("parallel", "arbitrary"),
        name="delta_scan_bwd" if reverse else "delta_scan_fwd",
    )(*args)
    return res if emit_state else (res[0], None)


def _rope(x, cosf, sins):
    return x * cosf + pltpu.roll(x, HEAD_DIM // 2, 1) * sins


def _rms_head(x, g):
    return x * lax.rsqrt(jnp.mean(x * x, axis=-1, keepdims=True) + EPS) * g


def _attn_kernel(*refs, band, has_ctx, use_sink, use_norm, use_rope, emit_k, has_alias, t_self, s_ctx, tq, grp,
                 key_chunk):
    it = iter(refs)
    sink_ref = next(it) if use_sink else None
    q_ref, k_ref, v_ref, z_ref = (next(it) for _ in range(4))
    if has_ctx:
        kc_ref, vc_ref = next(it), next(it)
    if use_norm:
        qn_ref, kn_ref = next(it), next(it)
    if use_rope:
        cq_ref, sq_ref, ck_ref, sk_ref = (next(it) for _ in range(4))
    if has_alias:
        next(it)
    y_ref = next(it)
    kout_ref = next(it) if emit_k else None
    ks_ref, vs_ref = next(it), next(it)

    h = pl.program_id(1)
    i = pl.program_id(2)
    pad = WINDOW if band else 0
    ctx0 = t_self + 2 * pad

    @pl.when(i == 0)
    def _():
        k = k_ref[...]
        if use_norm:
            k = _rms_head(k, kn_ref[...])
        if emit_k:
            kout_ref[...] = k
        if use_rope:
            k = _rope(k, ck_ref[...], sk_ref[...])
        ks_ref[pad:pad + t_self, :] = k.astype(BF16)
        vs_ref[pad:pad + t_self, :] = v_ref[...].astype(BF16)
        if band:
            zeros = jnp.zeros((pad, HEAD_DIM), BF16)
            ks_ref[0:pad, :] = zeros
            vs_ref[0:pad, :] = zeros
            ks_ref[pad + t_self:ctx0, :] = zeros
            vs_ref[pad + t_self:ctx0, :] = zeros
        if has_ctx:
            ks_ref[ctx0:ctx0 + s_ctx, :] = kc_ref[...].astype(BF16)
            vs_ref[ctx0:ctx0 + s_ctx, :] = vc_ref[...].astype(BF16)

    scale = HEAD_DIM ** -0.5 * LOG2E
    qs = []
    for g in range(grp):
        q = q_ref[:, g * HEAD_DIM:(g + 1) * HEAD_DIM]
        if use_norm:
            q = _rms_head(q, qn_ref[...])
        if use_rope:
            q = _rope(q, cq_ref[...], sq_ref[...])
        qs.append((q * scale).astype(BF16))
    q16 = jnp.concatenate(qs, axis=0)
    rows = grp * tq

    if band:
        r0 = pl.multiple_of(i * tq, tq)
        width = 3 * WINDOW
        qpos = lax.broadcasted_iota(jnp.int32, (rows, width), 0) & (tq - 1)
        krel = lax.broadcasted_iota(jnp.int32, (rows, width), 1) - WINDOW
        kpos = krel + i * tq
        valid = (jnp.abs(qpos - krel) <= WINDOW) & (kpos >= 0) & (kpos < t_self)
        pieces = [(lambda: jnp.where(valid, _dot_nt(q16, ks_ref[pl.ds(r0, width), :]), NEG),
                   lambda: vs_ref[pl.ds(r0, width), :])]
        bounds = [(ctx0, ctx0 + s_ctx)] if has_ctx else []
    else:
        step = min(t_self, key_chunk)
        bounds = [(a, a + step) for a in range(0, t_self, step)]
        if has_ctx:
            bounds.append((t_self, t_self + s_ctx))
        pieces = []
    for a, b in bounds:
        pieces.append((lambda a=a, b=b: _dot_nt(q16, ks_ref[a:b, :]), lambda a=a, b=b: vs_ref[a:b, :]))

    if use_sink:
        rid = lax.broadcasted_iota(jnp.int32, (rows, 1), 0)
        sink = jnp.zeros((rows, 1), F32)
        for g in range(grp):
            sink = jnp.where(rid >= g * tq, sink_ref[h * grp + g] * LOG2E, sink)
        m, denom, acc = sink, jnp.ones((rows, 1), F32), jnp.zeros((rows, HEAD_DIM), F32)
    else:
        m = denom = acc = None
    for scores, values in pieces:
        s = scores()
        m_new = jnp.max(s, axis=-1, keepdims=True)
        if m is not None:
            m_new = jnp.maximum(m, m_new)
        p = jnp.exp2(s - m_new)
        p_sum = jnp.sum(p, axis=-1, keepdims=True)
        pv = _dot(p.astype(BF16), values())
        if m is None:
            denom, acc = p_sum, pv
        else:
            alpha = jnp.exp2(m - m_new)
            denom = alpha * denom + p_sum
            acc = alpha * acc + pv
        m = m_new
    o = acc / denom
    for g in range(grp):
        cols = slice(g * HEAD_DIM, (g + 1) * HEAD_DIM)
        y_ref[:, cols] = (o[g * tq:(g + 1) * tq] * _silu(z_ref[:, cols])).astype(y_ref.dtype)


def _attention(p, *, q_block, k_block, v_block, z_block, seq_len, n_seq, row0, tq, heads, band, out,
               ctx_k=None, ctx_v=None, layer=None, sink_l=None, q_norm_l=None, k_norm_l=None, rope=None,
               emit_k=False):
    m = p.shape[0]
    bw = heads * HEAD_DIM
    grp = heads // KV_HEADS
    gw = grp * HEAD_DIM
    nq = seq_len // tq
    qblk0 = row0 // tq
    sblk0 = row0 // seq_len
    has_ctx = ctx_k is not None
    s_ctx = ctx_k.shape[2] if has_ctx else 0
    use_sink = sink_l is not None
    use_norm = q_norm_l is not None
    use_rope = rope is not None
    pad = WINDOW if band else 0
    assert seq_len <= KEY_CHUNK or seq_len % KEY_CHUNK == 0

    in_specs, args = [], []
    if use_sink:
        in_specs.append(pl.BlockSpec(memory_space=pltpu.SMEM))
        args.append(sink_l)
    in_specs += [
        pl.BlockSpec((tq, gw), lambda b, h, i: (qblk0 + b * nq + i, q_block + h)),
        pl.BlockSpec((seq_len, HEAD_DIM), lambda b, h, i: (sblk0 + b, k_block + h)),
        pl.BlockSpec((seq_len, HEAD_DIM), lambda b, h, i: (sblk0 + b, v_block + h)),
        pl.BlockSpec((tq, gw), lambda b, h, i: (qblk0 + b * nq + i, z_block + h)),
    ]
    args += [p, p, p, p]
    if has_ctx:
        spec = pl.BlockSpec((None, None, s_ctx, HEAD_DIM), lambda b, h, i: (b, layer, 0, h))
        in_specs += [spec, spec]
        args += [ctx_k, ctx_v]
    if use_norm:
        spec = pl.BlockSpec((1, HEAD_DIM), lambda b, h, i: (0, 0))
        in_specs += [spec, spec]
        args += [q_norm_l.reshape(1, HEAD_DIM), k_norm_l.reshape(1, HEAD_DIM)]
    if use_rope:
        cosf, sins = rope
        qspec = pl.BlockSpec((tq, HEAD_DIM), lambda b, h, i: (i, 0))
        kspec = pl.BlockSpec((seq_len, HEAD_DIM), lambda b, h, i: (0, 0))
        in_specs += [qspec, qspec, kspec, kspec]
        args += [cosf, sins, cosf, sins]
    in_specs, args, aliases = _with_alias(in_specs, args, out)
    out_specs = [pl.BlockSpec((tq, gw), lambda b, h, i: (qblk0 + b * nq + i, h))]
    out_shape = [jax.ShapeDtypeStruct((m, bw), BF16)]
    if emit_k:
        out_specs.append(pl.BlockSpec((None, seq_len, HEAD_DIM), lambda b, h, i: (b, 0, h)))
        out_shape.append(jax.ShapeDtypeStruct((n_seq, seq_len, KV_W), F32))
    kern = functools.partial(_attn_kernel, band=band, has_ctx=has_ctx, use_sink=use_sink, use_norm=use_norm,
                             use_rope=use_rope, emit_k=emit_k, has_alias=out is not None, t_self=seq_len,
                             s_ctx=s_ctx, tq=tq, grp=grp, key_chunk=KEY_CHUNK)
    n_keys = seq_len + 2 * pad + s_ctx
    assert tq & (tq - 1) == 0 and (not band or tq == WINDOW)
    res = pl.pallas_call(
        kern,
        grid=(n_seq, KV_HEADS, nq),
        in_specs=in_specs,
        out_specs=out_specs,
        out_shape=out_shape,
        scratch_shapes=[pltpu.VMEM((n_keys, HEAD_DIM), BF16), pltpu.VMEM((n_keys, HEAD_DIM), BF16)],
        input_output_aliases=aliases,
        compiler_params=_params("parallel", "parallel", "arbitrary"),
        name=("attn_band" if band else "attn_full") + ("_ctx" if has_ctx else ""),
    )(*args)
    return res if emit_k else (res[0], None)


def _merge_kernel(ya_ref, yb_ref, yc_ref, w_ref, ga_ref, gb_ref, gc_ref, o_ref):
    acc = jax.nn.sigmoid(ga_ref[...]) * _dot(ya_ref[...], w_ref[0])
    acc = acc + jax.nn.sigmoid(gb_ref[...]) * _dot(yb_ref[...], w_ref[1])
    acc = acc + jax.nn.sigmoid(gc_ref[...]) * _dot(yc_ref[...], w_ref[2])
    o_ref[...] = acc.astype(o_ref.dtype)


def _merge(ya, yb, yc, w_branch_l, p, gate_col0, tm, tn):
    m, bw = ya.shape
    d = w_branch_l.shape[2]
    g0 = gate_col0 // tn
    gstep = d // tn
    br = pl.BlockSpec((tm, bw), lambda i, j: (i, 0))
    return pl.pallas_call(
        _merge_kernel,
        grid=(m // tm, d // tn),
        in_specs=[
            br, br, br,
            pl.BlockSpec((N_BRANCH, bw, tn), lambda i, j: (0, 0, j)),
            pl.BlockSpec((tm, tn), lambda i, j: (i, g0 + j)),
            pl.BlockSpec((tm, tn), lambda i, j: (i, g0 + gstep + j)),
            pl.BlockSpec((tm, tn), lambda i, j: (i, g0 + 2 * gstep + j)),
        ],
        out_specs=pl.BlockSpec((tm, tn), lambda i, j: (i, j)),
        out_shape=jax.ShapeDtypeStruct((m, d), BF16),
        compiler_params=_params("parallel", "arbitrary"),
        name="merge",
    )(ya, yb, yc, w_branch_l, p, p, p)


def _out_kernel(a_ref, w_ref, x_ref, mod_ref, o_ref):
    o_ref[...] = x_ref[...] + mod_ref[2:3, :] * _dot(a_ref[...], w_ref[...])


def _out_proj(mixed, w_out_l, x, mod_l, m_ctx, t_lat, ctx_row, tm, tn):
    m, d = x.shape
    return pl.pallas_call(
        _out_kernel,
        grid=(m // tm, d // tn),
        in_specs=[
            pl.BlockSpec((tm, d), lambda i, j: (i, 0)),
            pl.BlockSpec((d, tn), lambda i, j: (0, j)),
            pl.BlockSpec((tm, tn), lambda i, j: (i, j)),
            pl.BlockSpec((None, 3, tn), lambda i, j: (_mod_row(i, tm, m_ctx, t_lat, ctx_row), 0, j)),
        ],
        out_specs=pl.BlockSpec((tm, tn), lambda i, j: (i, j)),
        out_shape=jax.ShapeDtypeStruct((m, d), F32),
        compiler_params=_params("parallel", "arbitrary"),
        name="out_proj",
    )(mixed, w_out_l, x, mod_l)


def _final_kernel(x_ref, g_ref, o_ref):
    x = x_ref[...]
    o_ref[...] = x * lax.rsqrt(jnp.mean(x * x, axis=-1, keepdims=True) + EPS) * g_ref[...]


def _final_norm(x, g, row0, n_rows, tile):
    d = x.shape[1]
    blk0 = row0 // tile
    return pl.pallas_call(
        _final_kernel,
        grid=(n_rows // tile,),
        in_specs=[pl.BlockSpec((tile, d), lambda i: (blk0 + i, 0)), pl.BlockSpec((1, d), lambda i: (0, 0))],
        out_specs=pl.BlockSpec((tile, d), lambda i: (i, 0)),
        out_shape=jax.ShapeDtypeStruct((n_rows, d), F32),
        compiler_params=_params("parallel"),
        name="final_norm",
    )(x, g.reshape(1, d))


def _axial_rope_tables(n_tokens):
    rows = n_tokens // GRID_W
    row = jnp.repeat(jnp.arange(rows), GRID_W).astype(F32)
    col = jnp.tile(jnp.arange(GRID_W), rows).astype(F32)
    n_freq = HEAD_DIM // 4
    inv = ROPE_THETA ** (-jnp.arange(n_freq, dtype=F32) / n_freq)
    ang = jnp.concatenate([row[:, None] * inv, col[:, None] * inv], axis=-1)
    cos, sin = jnp.cos(ang), jnp.sin(ang)
    return jnp.concatenate([cos, cos], axis=-1), jnp.concatenate([-sin, sin], axis=-1)


def kernel(x_prompt, x_sample, state_delta, cache_win_k, cache_win_v, cache_glob_k, cache_glob_v, c, c_ctx,
           norm_g, w_mod, b_mod, w_in, conv_w, a_log, dt_bias, norm_a, sink, q_norm, k_norm, w_branch, w_out,
           final_g):
    n_layers, d = norm_g.shape
    bw = d // 2
    heads = bw // HEAD_DIM
    nbc, s_len, _ = x_prompt.shape
    nbl, t_len, _ = x_sample.shape
    past = cache_win_k.shape[2]
    m_ctx, m_lat = nbc * s_len, nbl * t_len
    m = m_ctx + m_lat
    assert m_ctx % t_len == 0 and t_len % s_len == 0 and s_len % (2 * WINDOW) == 0 and 8 * heads <= LANES
    assert heads % KV_HEADS == 0 and heads % QUAD == 0 and bw % KV_W == 0

    widths = dict(a_qkv=3 * bw, a_z=bw, b_q=bw, b_k=KV_W, b_v=KV_W, b_z=bw, c_q=bw, c_k=KV_W, c_v=KV_W, c_z=bw,
                  gates=N_BRANCH * d)
    off, col = {}, 0
    for name, wd in widths.items():
        off[name] = col
        col += wd
    n_main = col
    n_ba = 4 * heads
    src = 4 * bw

    tile_seq = s_len
    tm_big = _pick_tile(math.gcd(m_ctx, t_len), (1024, 512, 256))
    tm_merge = _pick_tile(math.gcd(m_ctx, t_len), (512, 256))
    tn = _pick_tile(math.gcd(d, KV_W), (512,))
    tb = 4 * CHUNK

    w_main = jnp.concatenate([w_in[:, :, :src], w_in[:, :, src + n_ba:]], axis=-1).astype(BF16)
    w_ba = jnp.pad(w_in[:, :, src:src + n_ba], ((0, 0), (0, 0), (0, LANES - n_ba))).astype(BF16)
    w_branch16 = w_branch.astype(BF16)
    w_out16 = w_out.astype(BF16)

    ctx_row = nbl
    mod_rows = -(-(nbl + 1) // 8) * 8
    c_all = jnp.concatenate([c, c_ctx[None], jnp.zeros((mod_rows - nbl - 1, d), F32)], axis=0)
    mod = _modulation(c_all, w_mod, b_mod).reshape(n_layers, mod_rows, 3, d)

    rope = _axial_rope_tables(t_len)
    ckw = cache_win_k.reshape(nbl, n_layers, past, KV_W)
    cvw = cache_win_v.reshape(nbl, n_layers, past, KV_W)
    ckg = cache_glob_k.reshape(nbl, n_layers, past, KV_W)
    cvg = cache_glob_v.reshape(nbl, n_layers, past, KV_W)

    x = jnp.concatenate([x_prompt.reshape(m_ctx, d), x_sample.reshape(m_lat, d)], axis=0)
    states, win_k, win_v, glob_k, glob_v = [], [], [], [], []
    for l in range(n_layers):
        h = _prep(x, mod[l], norm_g[l], m_ctx, t_len, ctx_row, tile_seq)
        p = _matmul(h, w_main[l], tm_big, _pick_tile(n_main, (1024, tn)), F32, "in_proj")
        ba = _matmul(h, w_ba[l], tm_big, LANES, F32, "in_proj_gates")

        qkv = _delta_conv(p, conv_w[l], s_len, nbc, 0, None, bw)
        qkv = _delta_conv(p, conv_w[l], t_len, nbl, m_ctx // t_len, qkv, bw)
        gb = _delta_gates(ba, a_log[l], dt_bias[l], heads, tile_seq)
        gq = jnp.swapaxes(gb.reshape(m // CHUNK, CHUNK, LANES)[:, :, 2 * heads:4 * heads], 1, 2)
        gq = gq.reshape(m // CHUNK, 2 * heads // QUAD, QW)
        common = dict(heads=heads, tb=tb)
        fin = dict(p=p, z_block=off["a_z"] // bw, norm_a_l=norm_a[l])
        o_f, s_f = _delta_scan(qkv, gb, gq, seq_len=s_len, n_seq=nbc, row0=0, reverse=False,
                               emit_state=True, **common)
        o_f, _ = _delta_scan(qkv, gb, gq, seq_len=t_len, n_seq=nbl, row0=m_ctx, reverse=False, out=o_f,
                             s0=state_delta, s0_index=(l, 0), **common)
        ya, s_b = _delta_scan(qkv, gb, gq, seq_len=s_len, n_seq=nbc, row0=0, reverse=True,
                              o_prev=o_f, emit_state=True, **common, **fin)
        ya, _ = _delta_scan(qkv, gb, gq, seq_len=t_len, n_seq=nbl, row0=m_ctx, reverse=True, out=ya,
                            s0=state_delta, s0_index=(l, 1), o_prev=o_f, **common, **fin)
        states.append(jnp.stack([s_f, s_b], axis=1))

        gw = (heads // KV_HEADS) * HEAD_DIM
        blocks_b = dict(q_block=off["b_q"] // gw, k_block=off["b_k"] // HEAD_DIM, v_block=off["b_v"] // HEAD_DIM,
                        z_block=off["b_z"] // gw)
        blocks_c = dict(q_block=off["c_q"] // gw, k_block=off["c_k"] // HEAD_DIM, v_block=off["c_v"] // HEAD_DIM,
                        z_block=off["c_z"] // gw)
        ctx_args = dict(seq_len=s_len, n_seq=nbc, row0=0, tq=s_len, heads=heads, band=False, out=None)
        lat_args = dict(seq_len=t_len, n_seq=nbl, row0=m_ctx, tq=WINDOW, heads=heads, layer=l, rope=rope)
        yb, _ = _attention(p, sink_l=sink[l], **blocks_b, **ctx_args)
        yb, _ = _attention(p, out=yb, sink_l=sink[l], band=True, ctx_k=ckw, ctx_v=cvw, **blocks_b, **lat_args)
        yc, kn = _attention(p, q_norm_l=q_norm[l], k_norm_l=k_norm[l], emit_k=True, **blocks_c, **ctx_args)
        yc, _ = _attention(p, out=yc, q_norm_l=q_norm[l], k_norm_l=k_norm[l], band=False, ctx_k=ckg, ctx_v=cvg,
                           **blocks_c, **lat_args)

        def ctx_slab(name):
            return p[:m_ctx, off[name]:off[name] + KV_W].reshape(nbc, s_len, KV_HEADS, HEAD_DIM)

        win_k.append(ctx_slab("b_k"))
        win_v.append(ctx_slab("b_v"))
        glob_k.append(kn.reshape(nbc, s_len, KV_HEADS, HEAD_DIM))
        glob_v.append(ctx_slab("c_v"))

        mixed = _merge(ya, yb, yc, w_branch16[l], p, off["gates"], tm_merge, tn)
        x = _out_proj(mixed, w_out16[l], x, mod[l], m_ctx, t_len, ctx_row, tm_big, tn)

    y_prompt = _final_norm(x, final_g, 0, m_ctx, tile_seq).reshape(nbc, s_len, d)
    y_sample = _final_norm(x, final_g, m_ctx, m_lat, tile_seq).reshape(nbl, t_len, d)
    return (y_prompt, y_sample, jnp.stack(states, axis=1), jnp.stack(win_k, axis=1), jnp.stack(win_v, axis=1),
            jnp.stack(glob_k, axis=1), jnp.stack(glob_v, axis=1))
```

```python
import functools
import math

import jax
import jax.numpy as jnp
import numpy as np
from jax import lax
from jax.experimental import pallas as pl
from jax.experimental.pallas import tpu as pltpu

HEAD_DIM = 128
KV_HEADS = 4
KV_W = KV_HEADS * HEAD_DIM
CHUNK = 64
WINDOW = 128
GRID_W = 64
CONV_K = 3
N_BRANCH = 3
ROPE_THETA = 10000.0
EPS = 1e-6
NEG = -1e30
LOG2E = math.log2(math.e)
KEY_CHUNK = 1024

LANES = 128
VMEM_LIMIT_BYTES = 56 * 1024 * 1024

F32 = jnp.float32
BF16 = jnp.bfloat16


def _params(*sem):
    return pltpu.CompilerParams(dimension_semantics=sem, vmem_limit_bytes=VMEM_LIMIT_BYTES)


def _dot(a, b):
    return jnp.dot(a, b, preferred_element_type=F32)


def _dot_nt(a, b):
    return lax.dot_general(a, b, (((1,), (1,)), ((), ())), preferred_element_type=F32)


def _dot_tn(a, b):
    return lax.dot_general(a, b, (((0,), (0,)), ((), ())), preferred_element_type=F32)


def _silu(x):
    return x * jax.nn.sigmoid(x)


def _with_alias(in_specs, args, out):
    if out is None:
        return in_specs, args, {}
    return in_specs + [pl.BlockSpec(memory_space=pl.ANY)], args + [out], {len(args): 0}


def _pick_tile(n, candidates):
    for c in candidates:
        if n % c == 0:
            return c
    raise ValueError(f"no tile in {candidates} divides {n}")


def _mod_kernel(c_ref, w_ref, b_ref, o_ref):
    a = _silu(c_ref[...]).astype(BF16)
    o_ref[...] = _dot(a, w_ref[...].astype(BF16)) + b_ref[...]


def _modulation(c_all, w_mod, b_mod):
    n_layers, d, n3 = w_mod.shape
    rows = c_all.shape[0]
    tn = _pick_tile(n3, (512, 256, 128))
    return pl.pallas_call(
        _mod_kernel,
        grid=(n_layers, n3 // tn),
        in_specs=[
            pl.BlockSpec((rows, d), lambda l, j: (0, 0)),
            pl.BlockSpec((None, d, tn), lambda l, j: (l, 0, j)),
            pl.BlockSpec((None, 1, tn), lambda l, j: (l, 0, j)),
        ],
        out_specs=pl.BlockSpec((None, rows, tn), lambda l, j: (l, 0, j)),
        out_shape=jax.ShapeDtypeStruct((n_layers, rows, n3), F32),
        compiler_params=_params("parallel", "parallel"),
        name="modulation",
    )(c_all, w_mod, b_mod.reshape(n_layers, 1, n3))


def _prep_kernel(x_ref, mod_ref, g_ref, o_ref):
    x = x_ref[...]
    y = x * lax.rsqrt(jnp.mean(x * x, axis=-1, keepdims=True) + EPS) * g_ref[...]
    shift = mod_ref[0:1, :]
    scale = mod_ref[1:2, :]
    o_ref[...] = (y * (1.0 + scale) + shift).astype(o_ref.dtype)


def _mod_row(i, tile, m_ctx, t_lat, ctx_row):
    start = i * tile
    return jnp.where(start < m_ctx, ctx_row, (start - m_ctx) // t_lat)


def _prep(x, mod_l, norm_g_l, m_ctx, t_lat, ctx_row, tile):
    m, d = x.shape
    return pl.pallas_call(
        _prep_kernel,
        grid=(m // tile,),
        in_specs=[
            pl.BlockSpec((tile, d), lambda i: (i, 0)),
            pl.BlockSpec((None, 3, d), lambda i: (_mod_row(i, tile, m_ctx, t_lat, ctx_row), 0, 0)),
            pl.BlockSpec((1, d), lambda i: (0, 0)),
        ],
        out_specs=pl.BlockSpec((tile, d), lambda i: (i, 0)),
        out_shape=jax.ShapeDtypeStruct((m, d), BF16),
        compiler_params=_params("parallel"),
        name="prep",
    )(x, mod_l, norm_g_l.reshape(1, d))


def _mm_kernel(a_ref, w_ref, o_ref):
    o_ref[...] = _dot(a_ref[...], w_ref[...]).astype(o_ref.dtype)


def _matmul(a, w, tm, tn, out_dtype, name):
    m, k = a.shape
    _, n = w.shape
    return pl.pallas_call(
        _mm_kernel,
        grid=(m // tm, n // tn),
        in_specs=[
            pl.BlockSpec((tm, k), lambda i, j: (i, 0)),
            pl.BlockSpec((k, tn), lambda i, j: (0, j)),
        ],
        out_specs=pl.BlockSpec((tm, tn), lambda i, j: (i, j)),
        out_shape=jax.ShapeDtypeStruct((m, n), out_dtype),
        compiler_params=_params("parallel", "arbitrary"),
        name=name,
    )(a, w)


def _conv_kernel(x_ref, w_ref, *rest, n_qk_blocks):
    o_ref = rest[-1]
    x = x_ref[...]
    t = x.shape[0]
    row = lax.broadcasted_iota(jnp.int32, x.shape, 0)
    prev = jnp.where(row == 0, 0.0, pltpu.roll(x, 1, 0))
    nxt = jnp.where(row == t - 1, 0.0, pltpu.roll(x, t - 1, 0))
    y = _silu(prev * w_ref[0:1, :] + x * w_ref[1:2, :] + nxt * w_ref[2:3, :])
    is_qk = pl.program_id(1) < n_qk_blocks
    for s in range(x.shape[1] // HEAD_DIM):
        ys = y[:, s * HEAD_DIM:(s + 1) * HEAD_DIM]
        inv = lax.rsqrt(jnp.sum(ys * ys, axis=-1, keepdims=True) + EPS)
        o_ref[:, s * HEAD_DIM:(s + 1) * HEAD_DIM] = ys * jnp.where(is_qk, inv, 1.0)


CONV_BLOCK_BYTES = 4 * 1024 * 1024


def _delta_conv(p, conv_w_l, seq_len, n_seq, row_block0, out, bw):
    cols = HEAD_DIM
    while bw % (2 * cols) == 0 and seq_len * 2 * cols * 4 <= CONV_BLOCK_BYTES:
        cols *= 2
    n_col = 3 * bw // cols
    kern = functools.partial(_conv_kernel, n_qk_blocks=2 * bw // cols)
    m = p.shape[0]
    in_specs = [
        pl.BlockSpec((seq_len, cols), lambda b, j: (row_block0 + b, j)),
        pl.BlockSpec((CONV_K, cols), lambda b, j: (0, j)),
    ]
    args = [p, conv_w_l]
    in_specs, args, aliases = _with_alias(in_specs, args, out)
    return pl.pallas_call(
        kern,
        grid=(n_seq, n_col),
        in_specs=in_specs,
        out_specs=pl.BlockSpec((seq_len, cols), lambda b, j: (row_block0 + b, j)),
        out_shape=jax.ShapeDtypeStruct((m, 3 * bw), F32),
        input_output_aliases=aliases,
        compiler_params=_params("parallel", "parallel"),
        name="delta_conv",
    )(*args)


def _gates_kernel(ba_ref, alog_ref, dtb_ref, o_ref, *, heads):
    ba = ba_ref[...]
    rows = ba.shape[0]
    lane = lax.broadcasted_iota(jnp.int32, ba.shape, 1)
    pos = lax.broadcasted_iota(jnp.int32, ba.shape, 0) & (CHUNK - 1)
    beta = jax.nn.sigmoid(ba)
    z = ba + dtb_ref[...]
    softplus = jnp.maximum(z, 0.0) + jnp.log1p(jnp.exp(-jnp.abs(z)))
    g = -jnp.exp(alog_ref[...]) * softplus
    pre = g
    suf = g
    step = 1
    while step < CHUNK:
        pre = pre + jnp.where(pos >= step, pltpu.roll(pre, step, 0), 0.0)
        suf = suf + jnp.where(pos < CHUNK - step, pltpu.roll(suf, rows - step, 0), 0.0)
        step *= 2
    gc = jnp.where(lane < 3 * heads, pre, suf)
    total = pre + suf - g
    eg = jnp.exp(pltpu.roll(gc, 2 * heads, 1))
    ekd = jnp.exp(pltpu.roll(total - gc, 4 * heads, 1))
    o_ref[...] = jnp.where(lane < 2 * heads, beta,
                           jnp.where(lane < 4 * heads, gc, jnp.where(lane < 6 * heads, eg, ekd)))


def _delta_gates(ba, a_log_l, dt_bias_l, heads, tile):
    m = ba.shape[0]
    pad = jnp.zeros((LANES - 4 * heads,), F32)
    lead = jnp.zeros((2 * heads,), F32)
    alog = jnp.concatenate([lead, a_log_l.reshape(-1), pad]).reshape(1, LANES)
    dtb = jnp.concatenate([lead, dt_bias_l.reshape(-1), pad]).reshape(1, LANES)
    return pl.pallas_call(
        functools.partial(_gates_kernel, heads=heads),
        grid=(m // tile,),
        in_specs=[
            pl.BlockSpec((tile, LANES), lambda i: (i, 0)),
            pl.BlockSpec((1, LANES), lambda i: (0, 0)),
            pl.BlockSpec((1, LANES), lambda i: (0, 0)),
        ],
        out_specs=pl.BlockSpec((tile, LANES), lambda i: (i, 0)),
        out_shape=jax.ShapeDtypeStruct((m, LANES), F32),
        compiler_params=_params("parallel"),
        name="delta_gates",
    )(ba, alog, dtb)


QUAD = 4
QW = QUAD * CHUNK
QD = QUAD * HEAD_DIM


def _split_bf16(x):
    hi = x.astype(BF16)
    lo = (x - hi.astype(F32)).astype(BF16)
    return hi, lo


def _tile_rows(x):
    return jnp.concatenate([x] * QUAD, axis=0)


def _quad_inverse_step(t, p, bd_mask, last):
    t_hi, t_lo = _split_bf16(t)
    p_hi, p_lo = _split_bf16(p)
    if last:
        x_hi, x_lo = t_hi, t_lo
    else:
        x_hi = jnp.concatenate([t_hi, p_hi], axis=0)
        x_lo = jnp.concatenate([t_lo, p_lo], axis=0)
    bd_hi = _tile_rows(p_hi) * bd_mask
    bd_lo = _tile_rows(p_lo) * bd_mask
    lhs = jnp.concatenate([x_hi, x_lo, x_hi], axis=1)
    rhs = jnp.concatenate([bd_hi, bd_hi, bd_lo], axis=0)
    y = _dot(lhs, rhs)
    return t + y[:CHUNK], (None if last else y[CHUNK:])


def _delta_kernel(*refs, heads, reverse, has_s0, finalize, emit_state, has_alias):
    it = iter(refs)
    q_ref, k_ref, v_ref, gb_ref, gq_ref = (next(it) for _ in range(5))
    s0_ref = next(it) if has_s0 else None
    if finalize:
        oprev_ref, z_ref, na_ref = (next(it) for _ in range(3))
    if has_alias:
        next(it)
    o_ref = next(it)
    sout_ref = next(it) if emit_state else None
    s_ref = next(it)

    n = pl.program_id(1)
    n_chunks = q_ref.shape[0] // CHUNK
    n_quads = heads // QUAD
    direction = 1 if reverse else 0
    quads = range(n_quads)

    @pl.when(n == 0)
    def _():
        if has_s0:
            s_ref[...] = s0_ref[...]
        else:
            s_ref[...] = jnp.zeros_like(s_ref)

    ii = lax.broadcasted_iota(jnp.int32, (CHUNK, QW), 0)
    jq = lax.broadcasted_iota(jnp.int32, (CHUNK, QW), 1)
    jj = jq & (CHUNK - 1)
    if reverse:
        causal, strict = ii <= jj, ii < jj
    else:
        causal, strict = ii >= jj, ii > jj
    eye = jnp.where(ii == jj, 1.0, 0.0)
    br = lax.broadcasted_iota(jnp.int32, (QW, QW), 0) // CHUNK
    bd_mask = jnp.where(br == lax.broadcasted_iota(jnp.int32, (QW, QW), 1) // CHUNK, 1.0, 0.0).astype(BF16)
    kr = lax.broadcasted_iota(jnp.int32, (QW, QD), 0) // CHUNK
    k_mask = jnp.where(kr == lax.broadcasted_iota(jnp.int32, (QW, QD), 1) // HEAD_DIM, 1.0, 0.0).astype(BF16)
    zeros_s = jnp.zeros((HEAD_DIM, HEAD_DIM), BF16)
    scale = HEAD_DIM ** -0.5

    def chunk_body(ci, carry):
        c = (n_chunks - 1 - ci) if reverse else ci
        r0 = pl.multiple_of(c * CHUNK, CHUNK)
        rows = pl.ds(r0, CHUNK)
        gb = gb_ref[rows, :]
        gq = gq_ref[c]
        edge = 0 if reverse else CHUNK - 1

        def per_head(group, qd):
            l0 = group * 2 * heads + direction * heads + qd * QUAD
            return jnp.concatenate(
                [jnp.broadcast_to(gb[:, l0 + j:l0 + j + 1], (CHUNK, HEAD_DIM)) for j in range(QUAD)], axis=1)

        def per_head_narrow(group, qd):
            l0 = group * 2 * heads + direction * heads + qd * QUAD
            out = jnp.broadcast_to(gb[:, l0 + QUAD - 1:l0 + QUAD], (CHUNK, QW))
            for j in range(QUAD - 2, -1, -1):
                out = jnp.where(jq < (j + 1) * CHUNK, jnp.broadcast_to(gb[:, l0 + j:l0 + j + 1], (CHUNK, QW)), out)
            return out

        cols = [slice(qd * QD, (qd + 1) * QD) for qd in quads]
        k = [k_ref[rows, cols[qd]] for qd in quads]
        beta = [per_head(0, qd) for qd in quads]
        eg = [per_head(2, qd) for qd in quads]
        kb = [k[qd] * beta[qd] for qd in quads]
        k16 = [k[qd].astype(BF16) for qd in quads]
        qs = [q_ref[rows, cols[qd]] * scale for qd in quads]

        kq = [_dot_nt(jnp.concatenate([kb[qd].astype(BF16), qs[qd].astype(BF16)], axis=0),
                      _tile_rows(k16[qd]) * k_mask) for qd in quads]
        decay = []
        for qd in quads:
            grow = gq[direction * n_quads + qd:direction * n_quads + qd + 1, :]
            decay.append(jnp.where(causal, jnp.exp(jnp.minimum(per_head_narrow(1, qd) - grow, 0.0)), 0.0))
        qk16 = [(kq[qd][CHUNK:] * decay[qd]).astype(BF16) for qd in quads]

        t = [eye for _ in quads]
        p = [-jnp.where(strict, kq[qd][:CHUNK] * decay[qd], 0.0) for qd in quads]
        n_steps = CHUNK.bit_length() - 1
        for step in range(n_steps):
            res = [_quad_inverse_step(t[qd], p[qd], bd_mask, step == n_steps - 1) for qd in quads]
            t = [r[0] for r in res]
            p = [r[1] for r in res]

        uw = []
        for qd in quads:
            vb16 = (v_ref[rows, cols[qd]] * beta[qd]).astype(BF16)
            kbe16 = (kb[qd] * eg[qd]).astype(BF16)
            rhs = jnp.concatenate(
                [jnp.concatenate([vb16[:, j * HEAD_DIM:(j + 1) * HEAD_DIM],
                                  kbe16[:, j * HEAD_DIM:(j + 1) * HEAD_DIM]], axis=1) for j in range(QUAD)], axis=0)
            uw.append(_dot(_tile_rows(t[qd].astype(BF16)) * bd_mask, rhs))

        s_old = [s_ref[h] for h in range(heads)]
        v_new = [None] * heads
        o_inter = [None] * heads
        for qd in quads:
            qd16 = (qs[qd] * eg[qd]).astype(BF16)
            for pr in range(QUAD // 2):
                j0, j1 = 2 * pr, 2 * pr + 1
                h0, h1 = qd * QUAD + j0, qd * QUAD + j1
                w16 = jnp.concatenate([uw[qd][j0 * CHUNK:(j0 + 1) * CHUNK, HEAD_DIM:],
                                       uw[qd][j1 * CHUNK:(j1 + 1) * CHUNK, HEAD_DIM:]], axis=1).astype(BF16)
                lhs = jnp.concatenate([w16, qd16[:, j0 * HEAD_DIM:(j1 + 1) * HEAD_DIM]], axis=0)
                bd_s = jnp.concatenate(
                    [jnp.concatenate([s_old[h0].astype(BF16), zeros_s], axis=1),
                     jnp.concatenate([zeros_s, s_old[h1].astype(BF16)], axis=1)], axis=0)
                ws = _dot(lhs, bd_s)
                for j, h in ((j0, h0), (j1, h1)):
                    lanes = slice((j - j0) * HEAD_DIM, (j - j0 + 1) * HEAD_DIM)
                    v_new[h] = uw[qd][j * CHUNK:(j + 1) * CHUNK, :HEAD_DIM] - ws[:CHUNK, lanes]
                    o_inter[h] = ws[CHUNK:, lanes]
        v16 = [v_new[h].astype(BF16) for h in range(heads)]
        o_intra = [_dot(_tile_rows(qk16[qd]) * bd_mask,
                        jnp.concatenate(v16[qd * QUAD:(qd + 1) * QUAD], axis=0)) for qd in quads]

        for qd in quads:
            kd16 = (k[qd] * per_head(3, qd)).astype(BF16)
            for j in range(QUAD):
                h = qd * QUAD + j
                hc = slice(h * HEAD_DIM, (h + 1) * HEAD_DIM)
                lg = 4 * heads + direction * heads + h
                g_tot = gb[edge:edge + 1, lg:lg + 1]
                s_ref[h] = s_old[h] * g_tot + _dot_tn(kd16[:, j * HEAD_DIM:(j + 1) * HEAD_DIM], v16[h])
                o = o_inter[h] + o_intra[qd][j * CHUNK:(j + 1) * CHUNK]
                if finalize:
                    o = o + oprev_ref[rows, hc]
                    y = o * lax.rsqrt(jnp.mean(o * o, axis=-1, keepdims=True) + EPS) * na_ref[...]
                    o_ref[rows, hc] = (y * _silu(z_ref[rows, hc])).astype(o_ref.dtype)
                else:
                    o_ref[rows, hc] = o
        return carry

    lax.fori_loop(0, n_chunks, chunk_body, 0, unroll=2)

    if emit_state:
        @pl.when(n == pl.num_programs(1) - 1)
        def _():
            sout_ref[...] = s_ref[...]


def _delta_scan(qkv, gb, gq, *, heads, seq_len, n_seq, row0, tb, reverse, s0=None, s0_index=None,
                o_prev=None, p=None, z_block=None, norm_a_l=None, out=None, emit_state=False):
    bw = heads * HEAD_DIM
    m = qkv.shape[0]
    ntb = seq_len // tb
    blk0 = row0 // tb
    finalize = o_prev is not None

    def tok(b, n):
        step = (ntb - 1 - n) if reverse else n
        return blk0 + b * ntb + step

    in_specs = [
        pl.BlockSpec((tb, bw), lambda b, n: (tok(b, n), 0)),
        pl.BlockSpec((tb, bw), lambda b, n: (tok(b, n), 1)),
        pl.BlockSpec((tb, bw), lambda b, n: (tok(b, n), 2)),
        pl.BlockSpec((tb, LANES), lambda b, n: (tok(b, n), 0)),
        pl.BlockSpec((tb // CHUNK, 2 * heads // QUAD, QW), lambda b, n: (tok(b, n), 0, 0)),
    ]
    args = [qkv, qkv, qkv, gb, gq]
    if s0 is not None:
        layer, direction = s0_index
        in_specs.append(pl.BlockSpec((None, None, None, heads, HEAD_DIM, HEAD_DIM),
                                     lambda b, n: (b, layer, direction, 0, 0, 0)))
        args.append(s0)
    if finalize:
        in_specs += [
            pl.BlockSpec((tb, bw), lambda b, n: (tok(b, n), 0)),
            pl.BlockSpec((tb, bw), lambda b, n: (tok(b, n), z_block)),
            pl.BlockSpec((1, HEAD_DIM), lambda b, n: (0, 0)),
        ]
        args += [o_prev, p, norm_a_l.reshape(1, HEAD_DIM)]
    in_specs, args, aliases = _with_alias(in_specs, args, out)
    out_specs = [pl.BlockSpec((tb, bw), lambda b, n: (tok(b, n), 0))]
    out_shape = [jax.ShapeDtypeStruct((m, bw), BF16 if finalize else F32)]
    if emit_state:
        out_specs.append(pl.BlockSpec((None, heads, HEAD_DIM, HEAD_DIM), lambda b, n: (b, 0, 0, 0)))
        out_shape.append(jax.ShapeDtypeStruct((n_seq, heads, HEAD_DIM, HEAD_DIM), F32))
    kern = functools.partial(_delta_kernel, heads=heads, reverse=reverse, has_s0=s0 is not None,
                             finalize=finalize, emit_state=emit_state, has_alias=out is not None)
    res = pl.pallas_call(
        kern,
        grid=(n_seq, ntb),
        in_specs=in_specs,
        out_specs=out_specs,
        out_shape=out_shape,
        scratch_shapes=[pltpu.VMEM((heads, HEAD_DIM, HEAD_DIM), F32)],
        input_output_aliases=aliases,
        compiler_params=_params("parallel", "arbitrary"),
        name="delta_scan_bwd" if reverse else "delta_scan_fwd",
    )(*args)
    return res if emit_state else (res[0], None)


def _rope(x, cosf, sins):
    return x * cosf + pltpu.roll(x, HEAD_DIM // 2, 1) * sins


def _rms_head(x, g):
    return x * lax.rsqrt(jnp.mean(x * x, axis=-1, keepdims=True) + EPS) * g


def _attn_kernel(*refs, band, has_ctx, use_sink, use_norm, use_rope, emit_k, has_alias, t_self, s_ctx, tq, n_sub,
                 grp, key_chunk):
    it = iter(refs)
    sink_ref = next(it) if use_sink else None
    q_ref, k_ref, v_ref, z_ref = (next(it) for _ in range(4))
    if has_ctx:
        kc_ref, vc_ref = next(it), next(it)
    if use_norm:
        qn_ref, kn_ref = next(it), next(it)
    if use_rope:
        cq_ref, sq_ref, ck_ref, sk_ref = (next(it) for _ in range(4))
    if has_alias:
        next(it)
    y_ref = next(it)
    kout_ref = next(it) if emit_k else None
    ks_ref, vs_ref = next(it), next(it)

    h = pl.program_id(1)
    i = pl.program_id(2)
    pad = WINDOW if band else 0
    ctx0 = t_self + 2 * pad

    @pl.when(i == 0)
    def _():
        k = k_ref[...]
        if use_norm:
            k = _rms_head(k, kn_ref[...])
        if emit_k:
            kout_ref[...] = k
        if use_rope:
            k = _rope(k, ck_ref[...], sk_ref[...])
        def ones_lane(n_rows):
            return jnp.where(lax.broadcasted_iota(jnp.int32, (n_rows, HEAD_DIM), 1) == 0, 1.0, 0.0).astype(BF16)

        ks_ref[pad:pad + t_self, :] = k.astype(BF16)
        vs_ref[pad:pad + t_self, :HEAD_DIM] = v_ref[...].astype(BF16)
        vs_ref[pad:pad + t_self, HEAD_DIM:] = ones_lane(t_self)
        if band:
            ks_ref[0:pad, :] = jnp.zeros((pad, HEAD_DIM), BF16)
            vs_ref[0:pad, :] = jnp.zeros((pad, 2 * HEAD_DIM), BF16)
            ks_ref[pad + t_self:ctx0, :] = jnp.zeros((pad, HEAD_DIM), BF16)
            vs_ref[pad + t_self:ctx0, :] = jnp.zeros((pad, 2 * HEAD_DIM), BF16)
        if has_ctx:
            ks_ref[ctx0:ctx0 + s_ctx, :] = kc_ref[...].astype(BF16)
            vs_ref[ctx0:ctx0 + s_ctx, :HEAD_DIM] = vc_ref[...].astype(BF16)
            vs_ref[ctx0:ctx0 + s_ctx, HEAD_DIM:] = ones_lane(s_ctx)

    scale = HEAD_DIM ** -0.5 * LOG2E
    rows = grp * tq

    def q_block(sub):
        rs = slice(sub * tq, (sub + 1) * tq)
        qi = i * n_sub + sub
        qs = []
        for g in range(grp):
            q = q_ref[rs, g * HEAD_DIM:(g + 1) * HEAD_DIM]
            if use_norm:
                q = _rms_head(q, qn_ref[...])
            if use_rope:
                q = _rope(q, cq_ref[rs, :], sq_ref[rs, :])
            qs.append((q * scale).astype(BF16))
        q16 = jnp.concatenate(qs, axis=0)

        if band:
            r0 = pl.multiple_of(qi * tq, tq)
            width = 3 * WINDOW
            qpos = lax.broadcasted_iota(jnp.int32, (rows, width), 0) & (tq - 1)
            krel = lax.broadcasted_iota(jnp.int32, (rows, width), 1) - WINDOW
            kpos = krel + qi * tq
            valid = (jnp.abs(qpos - krel) <= WINDOW) & (kpos >= 0) & (kpos < t_self)
            pieces = [(lambda: jnp.where(valid, _dot_nt(q16, ks_ref[pl.ds(r0, width), :]), NEG),
                       lambda: vs_ref[pl.ds(r0, width), :])]
            bounds = [(ctx0, ctx0 + s_ctx)] if has_ctx else []
        else:
            step = min(t_self, key_chunk)
            bounds = [(a, a + step) for a in range(0, t_self, step)]
            if has_ctx:
                bounds.append((t_self, t_self + s_ctx))
            pieces = []
        for a, b in bounds:
            pieces.append((lambda a=a, b=b: _dot_nt(q16, ks_ref[a:b, :]), lambda a=a, b=b: vs_ref[a:b, :]))

        if use_sink:
            rid = lax.broadcasted_iota(jnp.int32, (rows, 1), 0)
            sink = jnp.zeros((rows, 1), F32)
            for g in range(grp):
                sink = jnp.where(rid >= g * tq, sink_ref[h * grp + g] * LOG2E, sink)
            m = sink
            acc = jnp.where(lax.broadcasted_iota(jnp.int32, (rows, 2 * HEAD_DIM), 1) == HEAD_DIM, 1.0, 0.0)
        else:
            m = acc = None
        for scores, values in pieces:
            s = scores()
            m_new = jnp.max(s, axis=-1, keepdims=True)
            if m is not None:
                m_new = jnp.maximum(m, m_new)
            pv = _dot(jnp.exp2(s - m_new).astype(BF16), values())
            acc = pv if m is None else jnp.exp2(m - m_new) * acc + pv
            m = m_new
        o = acc[:, :HEAD_DIM] / acc[:, HEAD_DIM:HEAD_DIM + 1]
        for g in range(grp):
            cols = slice(g * HEAD_DIM, (g + 1) * HEAD_DIM)
            y_ref[rs, cols] = (o[g * tq:(g + 1) * tq] * _silu(z_ref[rs, cols])).astype(y_ref.dtype)

    for sub in range(n_sub):
        q_block(sub)


def _attention(p, *, q_block, k_block, v_block, z_block, seq_len, n_seq, row0, tq, heads, band, out,
               ctx_k=None, ctx_v=None, layer=None, sink_l=None, q_norm_l=None, k_norm_l=None, rope=None,
               emit_k=False):
    m = p.shape[0]
    bw = heads * HEAD_DIM
    grp = heads // KV_HEADS
    gw = grp * HEAD_DIM
    n_sub = 2 if seq_len % (2 * tq) == 0 else 1
    tqs = n_sub * tq
    nq = seq_len // tqs
    qblk0 = row0 // tqs
    sblk0 = row0 // seq_len
    has_ctx = ctx_k is not None
    s_ctx = ctx_k.shape[2] if has_ctx else 0
    use_sink = sink_l is not None
    use_norm = q_norm_l is not None
    use_rope = rope is not None
    pad = WINDOW if band else 0
    assert seq_len <= KEY_CHUNK or seq_len % KEY_CHUNK == 0

    in_specs, args = [], []
    if use_sink:
        in_specs.append(pl.BlockSpec(memory_space=pltpu.SMEM))
        args.append(sink_l)
    in_specs += [
        pl.BlockSpec((tqs, gw), lambda b, h, i: (qblk0 + b * nq + i, q_block + h)),
        pl.BlockSpec((seq_len, HEAD_DIM), lambda b, h, i: (sblk0 + b, k_block + h)),
        pl.BlockSpec((seq_len, HEAD_DIM), lambda b, h, i: (sblk0 + b, v_block + h)),
        pl.BlockSpec((tqs, gw), lambda b, h, i: (qblk0 + b * nq + i, z_block + h)),
    ]
    args += [p, p, p, p]
    if has_ctx:
        spec = pl.BlockSpec((None, None, s_ctx, HEAD_DIM), lambda b, h, i: (b, layer, 0, h))
        in_specs += [spec, spec]
        args += [ctx_k, ctx_v]
    if use_norm:
        spec = pl.BlockSpec((1, HEAD_DIM), lambda b, h, i: (0, 0))
        in_specs += [spec, spec]
        args += [q_norm_l.reshape(1, HEAD_DIM), k_norm_l.reshape(1, HEAD_DIM)]
    if use_rope:
        cosf, sins = rope
        qspec = pl.BlockSpec((tqs, HEAD_DIM), lambda b, h, i: (i, 0))
        kspec = pl.BlockSpec((seq_len, HEAD_DIM), lambda b, h, i: (0, 0))
        in_specs += [qspec, qspec, kspec, kspec]
        args += [cosf, sins, cosf, sins]
    in_specs, args, aliases = _with_alias(in_specs, args, out)
    out_specs = [pl.BlockSpec((tqs, gw), lambda b, h, i: (qblk0 + b * nq + i, h))]
    out_shape = [jax.ShapeDtypeStruct((m, bw), BF16)]
    if emit_k:
        out_specs.append(pl.BlockSpec((None, seq_len, HEAD_DIM), lambda b, h, i: (b, 0, h)))
        out_shape.append(jax.ShapeDtypeStruct((n_seq, seq_len, KV_W), F32))
    kern = functools.partial(_attn_kernel, band=band, has_ctx=has_ctx, use_sink=use_sink, use_norm=use_norm,
                             use_rope=use_rope, emit_k=emit_k, has_alias=out is not None, t_self=seq_len,
                             s_ctx=s_ctx, tq=tq, n_sub=n_sub, grp=grp, key_chunk=KEY_CHUNK)
    n_keys = seq_len + 2 * pad + s_ctx
    assert tq & (tq - 1) == 0 and (not band or tq == WINDOW)
    res = pl.pallas_call(
        kern,
        grid=(n_seq, KV_HEADS, nq),
        in_specs=in_specs,
        out_specs=out_specs,
        out_shape=out_shape,
        scratch_shapes=[pltpu.VMEM((n_keys, HEAD_DIM), BF16), pltpu.VMEM((n_keys, 2 * HEAD_DIM), BF16)],
        input_output_aliases=aliases,
        compiler_params=_params("parallel", "parallel", "arbitrary"),
        name=("attn_band" if band else "attn_full") + ("_ctx" if has_ctx else ""),
    )(*args)
    return res if emit_k else (res[0], None)


def _merge_kernel(ya_ref, yb_ref, yc_ref, w_ref, ga_ref, gb_ref, gc_ref, o_ref):
    acc = jax.nn.sigmoid(ga_ref[...]) * _dot(ya_ref[...], w_ref[0])
    acc = acc + jax.nn.sigmoid(gb_ref[...]) * _dot(yb_ref[...], w_ref[1])
    acc = acc + jax.nn.sigmoid(gc_ref[...]) * _dot(yc_ref[...], w_ref[2])
    o_ref[...] = acc.astype(o_ref.dtype)


def _merge(ya, yb, yc, w_branch_l, p, gate_col0, tm, tn):
    m, bw = ya.shape
    d = w_branch_l.shape[2]
    g0 = gate_col0 // tn
    gstep = d // tn
    br = pl.BlockSpec((tm, bw), lambda i, j: (i, 0))
    return pl.pallas_call(
        _merge_kernel,
        grid=(m // tm, d // tn),
        in_specs=[
            br, br, br,
            pl.BlockSpec((N_BRANCH, bw, tn), lambda i, j: (0, 0, j)),
            pl.BlockSpec((tm, tn), lambda i, j: (i, g0 + j)),
            pl.BlockSpec((tm, tn), lambda i, j: (i, g0 + gstep + j)),
            pl.BlockSpec((tm, tn), lambda i, j: (i, g0 + 2 * gstep + j)),
        ],
        out_specs=pl.BlockSpec((tm, tn), lambda i, j: (i, j)),
        out_shape=jax.ShapeDtypeStruct((m, d), BF16),
        compiler_params=_params("parallel", "arbitrary"),
        name="merge",
    )(ya, yb, yc, w_branch_l, p, p, p)


def _out_kernel(a_ref, w_ref, x_ref, mod_ref, o_ref):
    o_ref[...] = x_ref[...] + mod_ref[2:3, :] * _dot(a_ref[...], w_ref[...])


def _out_proj(mixed, w_out_l, x, mod_l, m_ctx, t_lat, ctx_row, tm, tn):
    m, d = x.shape
    return pl.pallas_call(
        _out_kernel,
        grid=(m // tm, d // tn),
        in_specs=[
            pl.BlockSpec((tm, d), lambda i, j: (i, 0)),
            pl.BlockSpec((d, tn), lambda i, j: (0, j)),
            pl.BlockSpec((tm, tn), lambda i, j: (i, j)),
            pl.BlockSpec((None, 3, tn), lambda i, j: (_mod_row(i, tm, m_ctx, t_lat, ctx_row), 0, j)),
        ],
        out_specs=pl.BlockSpec((tm, tn), lambda i, j: (i, j)),
        out_shape=jax.ShapeDtypeStruct((m, d), F32),
        compiler_params=_params("parallel", "arbitrary"),
        name="out_proj",
    )(mixed, w_out_l, x, mod_l)


def _final_kernel(x_ref, g_ref, o_ref):
    x = x_ref[...]
    o_ref[...] = x * lax.rsqrt(jnp.mean(x * x, axis=-1, keepdims=True) + EPS) * g_ref[...]


def _final_norm(x, g, row0, n_rows, tile):
    d = x.shape[1]
    blk0 = row0 // tile
    return pl.pallas_call(
        _final_kernel,
        grid=(n_rows // tile,),
        in_specs=[pl.BlockSpec((tile, d), lambda i: (blk0 + i, 0)), pl.BlockSpec((1, d), lambda i: (0, 0))],
        out_specs=pl.BlockSpec((tile, d), lambda i: (i, 0)),
        out_shape=jax.ShapeDtypeStruct((n_rows, d), F32),
        compiler_params=_params("parallel"),
        name="final_norm",
    )(x, g.reshape(1, d))


def _axial_rope_tables(n_tokens):
    rows = n_tokens // GRID_W
    row = jnp.repeat(jnp.arange(rows), GRID_W).astype(F32)
    col = jnp.tile(jnp.arange(GRID_W), rows).astype(F32)
    n_freq = HEAD_DIM // 4
    inv = ROPE_THETA ** (-jnp.arange(n_freq, dtype=F32) / n_freq)
    ang = jnp.concatenate([row[:, None] * inv, col[:, None] * inv], axis=-1)
    cos, sin = jnp.cos(ang), jnp.sin(ang)
    return jnp.concatenate([cos, cos], axis=-1), jnp.concatenate([-sin, sin], axis=-1)


def kernel(x_prompt, x_sample, state_delta, cache_win_k, cache_win_v, cache_glob_k, cache_glob_v, c, c_ctx,
           norm_g, w_mod, b_mod, w_in, conv_w, a_log, dt_bias, norm_a, sink, q_norm, k_norm, w_branch, w_out,
           final_g):
    n_layers, d = norm_g.shape
    bw = d // 2
    heads = bw // HEAD_DIM
    nbc, s_len, _ = x_prompt.shape
    nbl, t_len, _ = x_sample.shape
    past = cache_win_k.shape[2]
    m_ctx, m_lat = nbc * s_len, nbl * t_len
    m = m_ctx + m_lat
    assert m_ctx % t_len == 0 and t_len % s_len == 0 and s_len % (2 * WINDOW) == 0 and 8 * heads <= LANES
    assert heads % KV_HEADS == 0 and heads % QUAD == 0 and bw % KV_W == 0

    widths = dict(a_qkv=3 * bw, a_z=bw, b_q=bw, b_k=KV_W, b_v=KV_W, b_z=bw, c_q=bw, c_k=KV_W, c_v=KV_W, c_z=bw,
                  gates=N_BRANCH * d)
    off, col = {}, 0
    for name, wd in widths.items():
        off[name] = col
        col += wd
    n_main = col
    n_ba = 4 * heads
    src = 4 * bw

    tile_seq = s_len
    tm_big = _pick_tile(math.gcd(m_ctx, t_len), (1024, 512, 256))
    tm_merge = _pick_tile(m, (1024, 512))
    tn = _pick_tile(math.gcd(d, KV_W), (512,))
    tn_wide = _pick_tile(math.gcd(d, n_main), (1024, tn))
    tb = 4 * CHUNK

    w_main = jnp.concatenate([w_in[:, :, :src], w_in[:, :, src + n_ba:]], axis=-1).astype(BF16)
    w_ba = jnp.pad(w_in[:, :, src:src + n_ba], ((0, 0), (0, 0), (0, LANES - n_ba))).astype(BF16)
    w_branch16 = w_branch.astype(BF16)
    w_out16 = w_out.astype(BF16)

    ctx_row = nbl
    mod_rows = -(-(nbl + 1) // 8) * 8
    c_all = jnp.concatenate([c, c_ctx[None], jnp.zeros((mod_rows - nbl - 1, d), F32)], axis=0)
    mod = _modulation(c_all, w_mod, b_mod).reshape(n_layers, mod_rows, 3, d)

    rope = _axial_rope_tables(t_len)
    ckw = cache_win_k.reshape(nbl, n_layers, past, KV_W)
    cvw = cache_win_v.reshape(nbl, n_layers, past, KV_W)
    ckg = cache_glob_k.reshape(nbl, n_layers, past, KV_W)
    cvg = cache_glob_v.reshape(nbl, n_layers, past, KV_W)

    x = jnp.concatenate([x_prompt.reshape(m_ctx, d), x_sample.reshape(m_lat, d)], axis=0)
    states, win_k, win_v, glob_k, glob_v = [], [], [], [], []
    for l in range(n_layers):
        h = _prep(x, mod[l], norm_g[l], m_ctx, t_len, ctx_row, tile_seq)
        p = _matmul(h, w_main[l], tm_big, tn_wide, F32, "in_proj")
        ba = _matmul(h, w_ba[l], tm_big, LANES, F32, "in_proj_gates")

        qkv = _delta_conv(p, conv_w[l], s_len, nbc, 0, None, bw)
        qkv = _delta_conv(p, conv_w[l], t_len, nbl, m_ctx // t_len, qkv, bw)
        gb = _delta_gates(ba, a_log[l], dt_bias[l], heads, tile_seq)
        gq = jnp.swapaxes(gb.reshape(m // CHUNK, CHUNK, LANES)[:, :, 2 * heads:4 * heads], 1, 2)
        gq = gq.reshape(m // CHUNK, 2 * heads // QUAD, QW)
        common = dict(heads=heads, tb=tb)
        fin = dict(p=p, z_block=off["a_z"] // bw, norm_a_l=norm_a[l])
        o_f, s_f = _delta_scan(qkv, gb, gq, seq_len=s_len, n_seq=nbc, row0=0, reverse=False,
                               emit_state=True, **common)
        o_f, _ = _delta_scan(qkv, gb, gq, seq_len=t_len, n_seq=nbl, row0=m_ctx, reverse=False, out=o_f,
                             s0=state_delta, s0_index=(l, 0), **common)
        ya, s_b = _delta_scan(qkv, gb, gq, seq_len=s_len, n_seq=nbc, row0=0, reverse=True,
                              o_prev=o_f, emit_state=True, **common, **fin)
        ya, _ = _delta_scan(qkv, gb, gq, seq_len=t_len, n_seq=nbl, row0=m_ctx, reverse=True, out=ya,
                            s0=state_delta, s0_index=(l, 1), o_prev=o_f, **common, **fin)
        states.append(jnp.stack([s_f, s_b], axis=1))

        gw = (heads // KV_HEADS) * HEAD_DIM
        blocks_b = dict(q_block=off["b_q"] // gw, k_block=off["b_k"] // HEAD_DIM, v_block=off["b_v"] // HEAD_DIM,
                        z_block=off["b_z"] // gw)
        blocks_c = dict(q_block=off["c_q"] // gw, k_block=off["c_k"] // HEAD_DIM, v_block=off["c_v"] // HEAD_DIM,
                        z_block=off["c_z"] // gw)
        ctx_args = dict(seq_len=s_len, n_seq=nbc, row0=0, tq=s_len, heads=heads, band=False, out=None)
        lat_args = dict(seq_len=t_len, n_seq=nbl, row0=m_ctx, tq=WINDOW, heads=heads, layer=l, rope=rope)
        yb, _ = _attention(p, sink_l=sink[l], **blocks_b, **ctx_args)
        yb, _ = _attention(p, out=yb, sink_l=sink[l], band=True, ctx_k=ckw, ctx_v=cvw, **blocks_b, **lat_args)
        yc, kn = _attention(p, q_norm_l=q_norm[l], k_norm_l=k_norm[l], emit_k=True, **blocks_c, **ctx_args)
        yc, _ = _attention(p, out=yc, q_norm_l=q_norm[l], k_norm_l=k_norm[l], band=False, ctx_k=ckg, ctx_v=cvg,
                           **blocks_c, **lat_args)

        def ctx_slab(name):
            return p[:m_ctx, off[name]:off[name] + KV_W].reshape(nbc, s_len, KV_HEADS, HEAD_DIM)

        win_k.append(ctx_slab("b_k"))
        win_v.append(ctx_slab("b_v"))
        glob_k.append(kn.reshape(nbc, s_len, KV_HEADS, HEAD_DIM))
        glob_v.append(ctx_slab("c_v"))

        mixed = _merge(ya, yb, yc, w_branch16[l], p, off["gates"], tm_merge, tn)
        x = _out_proj(mixed, w_out16[l], x, mod[l], m_ctx, t_len, ctx_row, tm_big, tn_wide)

    y_prompt = _final_norm(x, final_g, 0, m_ctx, tile_seq).reshape(nbc, s_len, d)
    y_sample = _final_norm(x, final_g, m_ctx, m_lat, tile_seq).reshape(nbl, t_len, d)
    return (y_prompt, y_sample, jnp.stack(states, axis=1), jnp.stack(win_k, axis=1), jnp.stack(win_v, axis=1),
            jnp.stack(glob_k, axis=1), jnp.stack(glob_v, axis=1))
```

```python
import functools
import math

import jax
import jax.numpy as jnp
import numpy as np
from jax import lax
from jax.experimental import pallas as pl
from jax.experimental.pallas import tpu as pltpu

HEAD_DIM = 128
KV_HEADS = 4
KV_W = KV_HEADS * HEAD_DIM
CHUNK = 64
WINDOW = 128
GRID_W = 64
CONV_K = 3
N_BRANCH = 3
ROPE_THETA = 10000.0
EPS = 1e-6
NEG = -1e30
LOG2E = math.log2(math.e)
KEY_CHUNK = 512

LANES = 128
VMEM_LIMIT_BYTES = 56 * 1024 * 1024

F32 = jnp.float32
BF16 = jnp.bfloat16


def _params(*sem):
    return pltpu.CompilerParams(dimension_semantics=sem, vmem_limit_bytes=VMEM_LIMIT_BYTES)


def _dot(a, b):
    return jnp.dot(a, b, preferred_element_type=F32)


def _dot_nt(a, b):
    return lax.dot_general(a, b, (((1,), (1,)), ((), ())), preferred_element_type=F32)


def _dot_tn(a, b):
    return lax.dot_general(a, b, (((0,), (0,)), ((), ())), preferred_element_type=F32)


def _silu(x):
    return x * jax.nn.sigmoid(x)


def _with_alias(in_specs, args, out):
    if out is None:
        return in_specs, args, {}
    return in_specs + [pl.BlockSpec(memory_space=pl.ANY)], args + [out], {len(args): 0}


def _pick_tile(n, candidates):
    for c in candidates:
        if n % c == 0:
            return c
    raise ValueError(f"no tile in {candidates} divides {n}")


def _mod_kernel(c_ref, w_ref, b_ref, o_ref):
    a = _silu(c_ref[...]).astype(BF16)
    o_ref[...] = _dot(a, w_ref[...].astype(BF16)) + b_ref[...]


def _modulation(c_all, w_mod, b_mod):
    n_layers, d, n3 = w_mod.shape
    rows = c_all.shape[0]
    tn = _pick_tile(n3, (1024, 512, 256, 128))
    return pl.pallas_call(
        _mod_kernel,
        grid=(n_layers, n3 // tn),
        in_specs=[
            pl.BlockSpec((rows, d), lambda l, j: (0, 0)),
            pl.BlockSpec((None, d, tn), lambda l, j: (l, 0, j)),
            pl.BlockSpec((None, 1, tn), lambda l, j: (l, 0, j)),
        ],
        out_specs=pl.BlockSpec((None, rows, tn), lambda l, j: (l, 0, j)),
        out_shape=jax.ShapeDtypeStruct((n_layers, rows, n3), F32),
        compiler_params=_params("parallel", "parallel"),
        name="modulation",
    )(c_all, w_mod, b_mod.reshape(n_layers, 1, n3))


def _prep_kernel(x_ref, mod_ref, g_ref, o_ref):
    x = x_ref[...]
    y = x * lax.rsqrt(jnp.mean(x * x, axis=-1, keepdims=True) + EPS) * g_ref[...]
    shift = mod_ref[0:1, :]
    scale = mod_ref[1:2, :]
    o_ref[...] = (y * (1.0 + scale) + shift).astype(o_ref.dtype)


def _mod_row(i, tile, m_ctx, t_lat, ctx_row):
    start = i * tile
    return jnp.where(start < m_ctx, ctx_row, (start - m_ctx) // t_lat)


def _prep(x, mod_l, norm_g_l, m_ctx, t_lat, ctx_row, tile):
    m, d = x.shape
    return pl.pallas_call(
        _prep_kernel,
        grid=(m // tile,),
        in_specs=[
            pl.BlockSpec((tile, d), lambda i: (i, 0)),
            pl.BlockSpec((None, 3, d), lambda i: (_mod_row(i, tile, m_ctx, t_lat, ctx_row), 0, 0)),
            pl.BlockSpec((1, d), lambda i: (0, 0)),
        ],
        out_specs=pl.BlockSpec((tile, d), lambda i: (i, 0)),
        out_shape=jax.ShapeDtypeStruct((m, d), BF16),
        compiler_params=_params("parallel"),
        name="prep",
    )(x, mod_l, norm_g_l.reshape(1, d))


def _mm_kernel(a_ref, w_ref, o_ref):
    o_ref[...] = _dot(a_ref[...], w_ref[...]).astype(o_ref.dtype)


def _matmul(a, w, tm, tn, out_dtype, name):
    m, k = a.shape
    _, n = w.shape
    return pl.pallas_call(
        _mm_kernel,
        grid=(m // tm, n // tn),
        in_specs=[
            pl.BlockSpec((tm, k), lambda i, j: (i, 0)),
            pl.BlockSpec((k, tn), lambda i, j: (0, j)),
        ],
        out_specs=pl.BlockSpec((tm, tn), lambda i, j: (i, j)),
        out_shape=jax.ShapeDtypeStruct((m, n), out_dtype),
        compiler_params=_params("parallel", "arbitrary"),
        name=name,
    )(a, w)


def _conv_kernel(x_ref, w_ref, *rest, n_qk_blocks):
    o_ref = rest[-1]
    x = x_ref[...]
    t = x.shape[0]
    row = lax.broadcasted_iota(jnp.int32, x.shape, 0)
    prev = jnp.where(row == 0, 0.0, pltpu.roll(x, 1, 0))
    nxt = jnp.where(row == t - 1, 0.0, pltpu.roll(x, t - 1, 0))
    y = _silu(prev * w_ref[0:1, :] + x * w_ref[1:2, :] + nxt * w_ref[2:3, :])
    is_qk = pl.program_id(1) < n_qk_blocks
    for s in range(x.shape[1] // HEAD_DIM):
        ys = y[:, s * HEAD_DIM:(s + 1) * HEAD_DIM]
        inv = lax.rsqrt(jnp.sum(ys * ys, axis=-1, keepdims=True) + EPS)
        o_ref[:, s * HEAD_DIM:(s + 1) * HEAD_DIM] = ys * jnp.where(is_qk, inv, 1.0)


CONV_BLOCK_BYTES = 4 * 1024 * 1024


def _delta_conv(p, conv_w_l, seq_len, n_seq, row_block0, out, bw):
    cols = HEAD_DIM
    while bw % (2 * cols) == 0 and seq_len * 2 * cols * 4 <= CONV_BLOCK_BYTES:
        cols *= 2
    n_col = 3 * bw // cols
    kern = functools.partial(_conv_kernel, n_qk_blocks=2 * bw // cols)
    m = p.shape[0]
    in_specs = [
        pl.BlockSpec((seq_len, cols), lambda b, j: (row_block0 + b, j)),
        pl.BlockSpec((CONV_K, cols), lambda b, j: (0, j)),
    ]
    args = [p, conv_w_l]
    in_specs, args, aliases = _with_alias(in_specs, args, out)
    return pl.pallas_call(
        kern,
        grid=(n_seq, n_col),
        in_specs=in_specs,
        out_specs=pl.BlockSpec((seq_len, cols), lambda b, j: (row_block0 + b, j)),
        out_shape=jax.ShapeDtypeStruct((m, 3 * bw), F32),
        input_output_aliases=aliases,
        compiler_params=_params("parallel", "parallel"),
        name="delta_conv",
    )(*args)


def _gates_kernel(ba_ref, alog_ref, dtb_ref, o_ref, *, heads):
    ba = ba_ref[...]
    rows = ba.shape[0]
    lane = lax.broadcasted_iota(jnp.int32, ba.shape, 1)
    pos = lax.broadcasted_iota(jnp.int32, ba.shape, 0) & (CHUNK - 1)
    beta = jax.nn.sigmoid(ba)
    z = ba + dtb_ref[...]
    softplus = jnp.maximum(z, 0.0) + jnp.log1p(jnp.exp(-jnp.abs(z)))
    g = -jnp.exp(alog_ref[...]) * softplus
    pre = g
    suf = g
    step = 1
    while step < CHUNK:
        pre = pre + jnp.where(pos >= step, pltpu.roll(pre, step, 0), 0.0)
        suf = suf + jnp.where(pos < CHUNK - step, pltpu.roll(suf, rows - step, 0), 0.0)
        step *= 2
    gc = jnp.where(lane < 3 * heads, pre, suf)
    total = pre + suf - g
    eg = jnp.exp(pltpu.roll(gc, 2 * heads, 1))
    ekd = jnp.exp(pltpu.roll(total - gc, 4 * heads, 1))
    o_ref[...] = jnp.where(lane < 2 * heads, beta,
                           jnp.where(lane < 4 * heads, gc, jnp.where(lane < 6 * heads, eg, ekd)))


def _delta_gates(ba, a_log_l, dt_bias_l, heads, tile):
    m = ba.shape[0]
    pad = jnp.zeros((LANES - 4 * heads,), F32)
    lead = jnp.zeros((2 * heads,), F32)
    alog = jnp.concatenate([lead, a_log_l.reshape(-1), pad]).reshape(1, LANES)
    dtb = jnp.concatenate([lead, dt_bias_l.reshape(-1), pad]).reshape(1, LANES)
    return pl.pallas_call(
        functools.partial(_gates_kernel, heads=heads),
        grid=(m // tile,),
        in_specs=[
            pl.BlockSpec((tile, LANES), lambda i: (i, 0)),
            pl.BlockSpec((1, LANES), lambda i: (0, 0)),
            pl.BlockSpec((1, LANES), lambda i: (0, 0)),
        ],
        out_specs=pl.BlockSpec((tile, LANES), lambda i: (i, 0)),
        out_shape=jax.ShapeDtypeStruct((m, LANES), F32),
        compiler_params=_params("parallel"),
        name="delta_gates",
    )(ba, alog, dtb)


QUAD = 4
QW = QUAD * CHUNK
QD = QUAD * HEAD_DIM


def _split_bf16(x):
    hi = x.astype(BF16)
    lo = (x - hi.astype(F32)).astype(BF16)
    return hi, lo


def _tile_rows(x):
    return jnp.concatenate([x] * QUAD, axis=0)


def _quad_inverse_step(t, p, bd_mask, first, last):
    p_hi, p_lo = _split_bf16(p)
    if first:
        x_hi, x_lo = p_hi, p_lo
    else:
        t_hi, t_lo = _split_bf16(t)
        x_hi = t_hi if last else jnp.concatenate([t_hi, p_hi], axis=0)
        x_lo = t_lo if last else jnp.concatenate([t_lo, p_lo], axis=0)
    bd_hi = _tile_rows(p_hi) * bd_mask
    bd_lo = _tile_rows(p_lo) * bd_mask
    lhs = jnp.concatenate([x_hi, x_lo, x_hi], axis=1)
    rhs = jnp.concatenate([bd_hi, bd_hi, bd_lo], axis=0)
    y = _dot(lhs, rhs)
    if first:
        return t + p, y
    return t + y[:CHUNK], (None if last else y[CHUNK:])


def _delta_kernel(*refs, heads, reverse, has_s0, finalize, emit_state, has_alias):
    it = iter(refs)
    q_ref, k_ref, v_ref, gb_ref, gq_ref = (next(it) for _ in range(5))
    s0_ref = next(it) if has_s0 else None
    if finalize:
        oprev_ref, z_ref, na_ref = (next(it) for _ in range(3))
    if has_alias:
        next(it)
    o_ref = next(it)
    sout_ref = next(it) if emit_state else None
    s_ref = next(it)

    n = pl.program_id(1)
    n_chunks = q_ref.shape[0] // CHUNK
    n_quads = heads // QUAD
    direction = 1 if reverse else 0
    quads = range(n_quads)

    @pl.when(n == 0)
    def _():
        if has_s0:
            s_ref[...] = s0_ref[...]
        else:
            s_ref[...] = jnp.zeros_like(s_ref)

    ii = lax.broadcasted_iota(jnp.int32, (CHUNK, QW), 0)
    jq = lax.broadcasted_iota(jnp.int32, (CHUNK, QW), 1)
    jj = jq & (CHUNK - 1)
    if reverse:
        causal, strict = ii <= jj, ii < jj
    else:
        causal, strict = ii >= jj, ii > jj
    eye = jnp.where(ii == jj, 1.0, 0.0)
    br = lax.broadcasted_iota(jnp.int32, (QW, QW), 0) // CHUNK
    bd_mask = jnp.where(br == lax.broadcasted_iota(jnp.int32, (QW, QW), 1) // CHUNK, 1.0, 0.0).astype(BF16)
    kr = lax.broadcasted_iota(jnp.int32, (QW, QD), 0) // CHUNK
    k_mask = jnp.where(kr == lax.broadcasted_iota(jnp.int32, (QW, QD), 1) // HEAD_DIM, 1.0, 0.0).astype(BF16)
    zeros_s = jnp.zeros((HEAD_DIM, HEAD_DIM), BF16)
    scale = HEAD_DIM ** -0.5

    def chunk_body(ci, carry):
        c = (n_chunks - 1 - ci) if reverse else ci
        r0 = pl.multiple_of(c * CHUNK, CHUNK)
        rows = pl.ds(r0, CHUNK)
        gb = gb_ref[rows, :]
        gq = gq_ref[c]
        edge = 0 if reverse else CHUNK - 1

        def per_head(group, qd):
            l0 = group * 2 * heads + direction * heads + qd * QUAD
            return jnp.concatenate(
                [jnp.broadcast_to(gb[:, l0 + j:l0 + j + 1], (CHUNK, HEAD_DIM)) for j in range(QUAD)], axis=1)

        def per_head_narrow(group, qd):
            l0 = group * 2 * heads + direction * heads + qd * QUAD
            out = jnp.broadcast_to(gb[:, l0 + QUAD - 1:l0 + QUAD], (CHUNK, QW))
            for j in range(QUAD - 2, -1, -1):
                out = jnp.where(jq < (j + 1) * CHUNK, jnp.broadcast_to(gb[:, l0 + j:l0 + j + 1], (CHUNK, QW)), out)
            return out

        cols = [slice(qd * QD, (qd + 1) * QD) for qd in quads]
        k = [k_ref[rows, cols[qd]] for qd in quads]
        beta = [per_head(0, qd) for qd in quads]
        eg = [per_head(2, qd) for qd in quads]
        kb = [k[qd] * beta[qd] for qd in quads]
        k16 = [k[qd].astype(BF16) for qd in quads]
        qs = [q_ref[rows, cols[qd]] * scale for qd in quads]

        kq = [_dot_nt(jnp.concatenate([kb[qd].astype(BF16), qs[qd].astype(BF16)], axis=0),
                      _tile_rows(k16[qd]) * k_mask) for qd in quads]
        decay = []
        for qd in quads:
            grow = gq[direction * n_quads + qd:direction * n_quads + qd + 1, :]
            decay.append(jnp.where(causal, jnp.exp(jnp.minimum(per_head_narrow(1, qd) - grow, 0.0)), 0.0))
        qk16 = [(kq[qd][CHUNK:] * decay[qd]).astype(BF16) for qd in quads]

        t = [eye for _ in quads]
        p = [-jnp.where(strict, kq[qd][:CHUNK] * decay[qd], 0.0) for qd in quads]
        n_steps = CHUNK.bit_length() - 1
        for step in range(n_steps):
            res = [_quad_inverse_step(t[qd], p[qd], bd_mask, step == 0, step == n_steps - 1) for qd in quads]
            t = [r[0] for r in res]
            p = [r[1] for r in res]

        uw = []
        for qd in quads:
            vb16 = (v_ref[rows, cols[qd]] * beta[qd]).astype(BF16)
            kbe16 = (kb[qd] * eg[qd]).astype(BF16)
            rhs = jnp.concatenate(
                [jnp.concatenate([vb16[:, j * HEAD_DIM:(j + 1) * HEAD_DIM],
                                  kbe16[:, j * HEAD_DIM:(j + 1) * HEAD_DIM]], axis=1) for j in range(QUAD)], axis=0)
            uw.append(_dot(_tile_rows(t[qd].astype(BF16)) * bd_mask, rhs))

        s_old = [s_ref[h] for h in range(heads)]
        v_new = [None] * heads
        o_inter = [None] * heads
        for qd in quads:
            qd16 = (qs[qd] * eg[qd]).astype(BF16)
            for pr in range(QUAD // 2):
                j0, j1 = 2 * pr, 2 * pr + 1
                h0, h1 = qd * QUAD + j0, qd * QUAD + j1
                w16 = jnp.concatenate([uw[qd][j0 * CHUNK:(j0 + 1) * CHUNK, HEAD_DIM:],
                                       uw[qd][j1 * CHUNK:(j1 + 1) * CHUNK, HEAD_DIM:]], axis=1).astype(BF16)
                lhs = jnp.concatenate([w16, qd16[:, j0 * HEAD_DIM:(j1 + 1) * HEAD_DIM]], axis=0)
                bd_s = jnp.concatenate(
                    [jnp.concatenate([s_old[h0].astype(BF16), zeros_s], axis=1),
                     jnp.concatenate([zeros_s, s_old[h1].astype(BF16)], axis=1)], axis=0)
                ws = _dot(lhs, bd_s)
                for j, h in ((j0, h0), (j1, h1)):
                    lanes = slice((j - j0) * HEAD_DIM, (j - j0 + 1) * HEAD_DIM)
                    v_new[h] = uw[qd][j * CHUNK:(j + 1) * CHUNK, :HEAD_DIM] - ws[:CHUNK, lanes]
                    o_inter[h] = ws[CHUNK:, lanes]
        v16 = [v_new[h].astype(BF16) for h in range(heads)]
        o_intra = [_dot(_tile_rows(qk16[qd]) * bd_mask,
                        jnp.concatenate(v16[qd * QUAD:(qd + 1) * QUAD], axis=0)) for qd in quads]

        for qd in quads:
            kd16 = (k[qd] * per_head(3, qd)).astype(BF16)
            for j in range(QUAD):
                h = qd * QUAD + j
                hc = slice(h * HEAD_DIM, (h + 1) * HEAD_DIM)
                lg = 4 * heads + direction * heads + h
                g_tot = gb[edge:edge + 1, lg:lg + 1]
                s_ref[h] = s_old[h] * g_tot + _dot_tn(kd16[:, j * HEAD_DIM:(j + 1) * HEAD_DIM], v16[h])
                o = o_inter[h] + o_intra[qd][j * CHUNK:(j + 1) * CHUNK]
                if finalize:
                    o = o + oprev_ref[rows, hc]
                    y = o * lax.rsqrt(jnp.mean(o * o, axis=-1, keepdims=True) + EPS) * na_ref[...]
                    o_ref[rows, hc] = (y * _silu(z_ref[rows, hc])).astype(o_ref.dtype)
                else:
                    o_ref[rows, hc] = o
        return carry

    lax.fori_loop(0, n_chunks, chunk_body, 0, unroll=2)

    if emit_state:
        @pl.when(n == pl.num_programs(1) - 1)
        def _():
            sout_ref[...] = s_ref[...]


def _delta_scan(qkv, gb, gq, *, heads, seq_len, n_seq, row0, tb, reverse, s0=None, s0_index=None,
                o_prev=None, p=None, z_block=None, norm_a_l=None, out=None, emit_state=False):
    bw = heads * HEAD_DIM
    m = qkv.shape[0]
    ntb = seq_len // tb
    blk0 = row0 // tb
    finalize = o_prev is not None

    def tok(b, n):
        step = (ntb - 1 - n) if reverse else n
        return blk0 + b * ntb + step

    in_specs = [
        pl.BlockSpec((tb, bw), lambda b, n: (tok(b, n), 0)),
        pl.BlockSpec((tb, bw), lambda b, n: (tok(b, n), 1)),
        pl.BlockSpec((tb, bw), lambda b, n: (tok(b, n), 2)),
        pl.BlockSpec((tb, LANES), lambda b, n: (tok(b, n), 0)),
        pl.BlockSpec((tb // CHUNK, 2 * heads // QUAD, QW), lambda b, n: (tok(b, n), 0, 0)),
    ]
    args = [qkv, qkv, qkv, gb, gq]
    if s0 is not None:
        layer, direction = s0_index
        in_specs.append(pl.BlockSpec((None, None, None, heads, HEAD_DIM, HEAD_DIM),
                                     lambda b, n: (b, layer, direction, 0, 0, 0)))
        args.append(s0)
    if finalize:
        in_specs += [
            pl.BlockSpec((tb, bw), lambda b, n: (tok(b, n), 0)),
            pl.BlockSpec((tb, bw), lambda b, n: (tok(b, n), z_block)),
            pl.BlockSpec((1, HEAD_DIM), lambda b, n: (0, 0)),
        ]
        args += [o_prev, p, norm_a_l.reshape(1, HEAD_DIM)]
    in_specs, args, aliases = _with_alias(in_specs, args, out)
    out_specs = [pl.BlockSpec((tb, bw), lambda b, n: (tok(b, n), 0))]
    out_shape = [jax.ShapeDtypeStruct((m, bw), BF16 if finalize else F32)]
    if emit_state:
        out_specs.append(pl.BlockSpec((None, heads, HEAD_DIM, HEAD_DIM), lambda b, n: (b, 0, 0, 0)))
        out_shape.append(jax.ShapeDtypeStruct((n_seq, heads, HEAD_DIM, HEAD_DIM), F32))
    kern = functools.partial(_delta_kernel, heads=heads, reverse=reverse, has_s0=s0 is not None,
                             finalize=finalize, emit_state=emit_state, has_alias=out is not None)
    res = pl.pallas_call(
        kern,
        grid=(n_seq, ntb),
        in_specs=in_specs,
        out_specs=out_specs,
        out_shape=out_shape,
        scratch_shapes=[pltpu.VMEM((heads, HEAD_DIM, HEAD_DIM), F32)],
        input_output_aliases=aliases,
        compiler_params=_params("parallel", "arbitrary"),
        name="delta_scan_bwd" if reverse else "delta_scan_fwd",
    )(*args)
    return res if emit_state else (res[0], None)


def _rope(x, cosf, sins):
    return x * cosf + pltpu.roll(x, HEAD_DIM // 2, 1) * sins


def _rms_head(x, g):
    return x * lax.rsqrt(jnp.mean(x * x, axis=-1, keepdims=True) + EPS) * g


def _attn_kernel(*refs, band, has_ctx, use_sink, use_norm, use_rope, emit_k, has_alias, t_self, s_ctx, tq, n_sub,
                 grp, key_chunk):
    it = iter(refs)
    sink_ref = next(it) if use_sink else None
    q_ref, k_ref, v_ref, z_ref = (next(it) for _ in range(4))
    if has_ctx:
        kc_ref, vc_ref = next(it), next(it)
    if use_norm:
        qn_ref, kn_ref = next(it), next(it)
    if use_rope:
        cq_ref, sq_ref, ck_ref, sk_ref = (next(it) for _ in range(4))
    if has_alias:
        next(it)
    y_ref = next(it)
    kout_ref = next(it) if emit_k else None
    ks_ref, vs_ref = next(it), next(it)

    h = pl.program_id(1)
    i = pl.program_id(2)
    pad = WINDOW if band else 0
    ctx0 = t_self + 2 * pad

    @pl.when(i == 0)
    def _():
        k = k_ref[...]
        if use_norm:
            k = _rms_head(k, kn_ref[...])
        if emit_k:
            kout_ref[...] = k
        if use_rope:
            k = _rope(k, ck_ref[...], sk_ref[...])
        def ones_lane(n_rows):
            return jnp.where(lax.broadcasted_iota(jnp.int32, (n_rows, HEAD_DIM), 1) == 0, 1.0, 0.0).astype(BF16)

        ks_ref[pad:pad + t_self, :] = k.astype(BF16)
        vs_ref[pad:pad + t_self, :HEAD_DIM] = v_ref[...].astype(BF16)
        vs_ref[pad:pad + t_self, HEAD_DIM:] = ones_lane(t_self)
        if band:
            ks_ref[0:pad, :] = jnp.zeros((pad, HEAD_DIM), BF16)
            vs_ref[0:pad, :] = jnp.zeros((pad, 2 * HEAD_DIM), BF16)
            ks_ref[pad + t_self:ctx0, :] = jnp.zeros((pad, HEAD_DIM), BF16)
            vs_ref[pad + t_self:ctx0, :] = jnp.zeros((pad, 2 * HEAD_DIM), BF16)
        if has_ctx:
            ks_ref[ctx0:ctx0 + s_ctx, :] = kc_ref[...].astype(BF16)
            vs_ref[ctx0:ctx0 + s_ctx, :HEAD_DIM] = vc_ref[...].astype(BF16)
            vs_ref[ctx0:ctx0 + s_ctx, HEAD_DIM:] = ones_lane(s_ctx)

    scale = HEAD_DIM ** -0.5 * LOG2E
    rows = grp * tq
    if band:
        width = 3 * WINDOW
        krel = lax.broadcasted_iota(jnp.int32, (rows, width), 1) - WINDOW
        in_band = jnp.abs((lax.broadcasted_iota(jnp.int32, (rows, width), 0) & (tq - 1)) - krel) <= WINDOW

    def q_block(sub):
        rs = slice(sub * tq, (sub + 1) * tq)
        qi = i * n_sub + sub
        qs = []
        for g in range(grp):
            q = q_ref[rs, g * HEAD_DIM:(g + 1) * HEAD_DIM]
            if use_norm:
                q = _rms_head(q, qn_ref[...])
            if use_rope:
                q = _rope(q, cq_ref[rs, :], sq_ref[rs, :])
            qs.append((q * scale).astype(BF16))
        q16 = jnp.concatenate(qs, axis=0)

        if band:
            r0 = pl.multiple_of(qi * tq, tq)
            kpos = krel + qi * tq
            valid = in_band & (kpos >= 0) & (kpos < t_self)
            pieces = [(lambda: jnp.where(valid, _dot_nt(q16, ks_ref[pl.ds(r0, width), :]), NEG),
                       lambda: vs_ref[pl.ds(r0, width), :])]
            bounds = [(ctx0, ctx0 + s_ctx)] if has_ctx else []
        else:
            step = min(t_self, key_chunk)
            bounds = [(a, a + step) for a in range(0, t_self, step)]
            if has_ctx:
                bounds.append((t_self, t_self + s_ctx))
            pieces = []
        for a, b in bounds:
            pieces.append((lambda a=a, b=b: _dot_nt(q16, ks_ref[a:b, :]), lambda a=a, b=b: vs_ref[a:b, :]))

        if use_sink:
            rid = lax.broadcasted_iota(jnp.int32, (rows, 1), 0)
            sink = jnp.zeros((rows, 1), F32)
            for g in range(grp):
                sink = jnp.where(rid >= g * tq, sink_ref[h * grp + g] * LOG2E, sink)
            m = sink
            acc = jnp.where(lax.broadcasted_iota(jnp.int32, (rows, 2 * HEAD_DIM), 1) == HEAD_DIM, 1.0, 0.0)
        else:
            m = acc = None
        for scores, values in pieces:
            s = scores()
            m_new = jnp.max(s, axis=-1, keepdims=True)
            if m is not None:
                m_new = jnp.maximum(m, m_new)
            pv = _dot(jnp.exp2(s - m_new).astype(BF16), values())
            acc = pv if m is None else jnp.exp2(m - m_new) * acc + pv
            m = m_new
        o = acc[:, :HEAD_DIM] / acc[:, HEAD_DIM:HEAD_DIM + 1]
        for g in range(grp):
            cols = slice(g * HEAD_DIM, (g + 1) * HEAD_DIM)
            y_ref[rs, cols] = (o[g * tq:(g + 1) * tq] * _silu(z_ref[rs, cols])).astype(y_ref.dtype)

    for sub in range(n_sub):
        q_block(sub)


def _attention(p, *, q_block, k_block, v_block, z_block, seq_len, n_seq, row0, tq, heads, band, out,
               ctx_k=None, ctx_v=None, layer=None, sink_l=None, q_norm_l=None, k_norm_l=None, rope=None,
               emit_k=False):
    m = p.shape[0]
    bw = heads * HEAD_DIM
    grp = heads // KV_HEADS
    gw = grp * HEAD_DIM
    n_sub = 4 if band else 2
    while seq_len % (n_sub * tq):
        n_sub //= 2
    tqs = n_sub * tq
    nq = seq_len // tqs
    qblk0 = row0 // tqs
    sblk0 = row0 // seq_len
    has_ctx = ctx_k is not None
    s_ctx = ctx_k.shape[2] if has_ctx else 0
    use_sink = sink_l is not None
    use_norm = q_norm_l is not None
    use_rope = rope is not None
    pad = WINDOW if band else 0
    assert seq_len <= KEY_CHUNK or seq_len % KEY_CHUNK == 0

    in_specs, args = [], []
    if use_sink:
        in_specs.append(pl.BlockSpec(memory_space=pltpu.SMEM))
        args.append(sink_l)
    in_specs += [
        pl.BlockSpec((tqs, gw), lambda b, h, i: (qblk0 + b * nq + i, q_block + h)),
        pl.BlockSpec((seq_len, HEAD_DIM), lambda b, h, i: (sblk0 + b, k_block + h)),
        pl.BlockSpec((seq_len, HEAD_DIM), lambda b, h, i: (sblk0 + b, v_block + h)),
        pl.BlockSpec((tqs, gw), lambda b, h, i: (qblk0 + b * nq + i, z_block + h)),
    ]
    args += [p, p, p, p]
    if has_ctx:
        spec = pl.BlockSpec((None, None, s_ctx, HEAD_DIM), lambda b, h, i: (b, layer, 0, h))
        in_specs += [spec, spec]
        args += [ctx_k, ctx_v]
    if use_norm:
        spec = pl.BlockSpec((1, HEAD_DIM), lambda b, h, i: (0, 0))
        in_specs += [spec, spec]
        args += [q_norm_l.reshape(1, HEAD_DIM), k_norm_l.reshape(1, HEAD_DIM)]
    if use_rope:
        cosf, sins = rope
        qspec = pl.BlockSpec((tqs, HEAD_DIM), lambda b, h, i: (i, 0))
        kspec = pl.BlockSpec((seq_len, HEAD_DIM), lambda b, h, i: (0, 0))
        in_specs += [qspec, qspec, kspec, kspec]
        args += [cosf, sins, cosf, sins]
    in_specs, args, aliases = _with_alias(in_specs, args, out)
    out_specs = [pl.BlockSpec((tqs, gw), lambda b, h, i: (qblk0 + b * nq + i, h))]
    out_shape = [jax.ShapeDtypeStruct((m, bw), BF16)]
    if emit_k:
        out_specs.append(pl.BlockSpec((None, seq_len, HEAD_DIM), lambda b, h, i: (b, 0, h)))
        out_shape.append(jax.ShapeDtypeStruct((n_seq, seq_len, KV_W), F32))
    kern = functools.partial(_attn_kernel, band=band, has_ctx=has_ctx, use_sink=use_sink, use_norm=use_norm,
                             use_rope=use_rope, emit_k=emit_k, has_alias=out is not None, t_self=seq_len,
                             s_ctx=s_ctx, tq=tq, n_sub=n_sub, grp=grp, key_chunk=KEY_CHUNK)
    n_keys = seq_len + 2 * pad + s_ctx
    assert tq & (tq - 1) == 0 and (not band or tq == WINDOW)
    res = pl.pallas_call(
        kern,
        grid=(n_seq, KV_HEADS, nq),
        in_specs=in_specs,
        out_specs=out_specs,
        out_shape=out_shape,
        scratch_shapes=[pltpu.VMEM((n_keys, HEAD_DIM), BF16), pltpu.VMEM((n_keys, 2 * HEAD_DIM), BF16)],
        input_output_aliases=aliases,
        compiler_params=_params("parallel", "parallel", "arbitrary"),
        name=("attn_band" if band else "attn_full") + ("_ctx" if has_ctx else ""),
    )(*args)
    return res if emit_k else (res[0], None)


def _merge_kernel(ya_ref, yb_ref, yc_ref, w_ref, ga_ref, gb_ref, gc_ref, o_ref):
    acc = jax.nn.sigmoid(ga_ref[...]) * _dot(ya_ref[...], w_ref[0])
    acc = acc + jax.nn.sigmoid(gb_ref[...]) * _dot(yb_ref[...], w_ref[1])
    acc = acc + jax.nn.sigmoid(gc_ref[...]) * _dot(yc_ref[...], w_ref[2])
    o_ref[...] = acc.astype(o_ref.dtype)


def _merge(ya, yb, yc, w_branch_l, p, gate_col0, tm, tn):
    m, bw = ya.shape
    d = w_branch_l.shape[2]
    g0 = gate_col0 // tn
    gstep = d // tn
    br = pl.BlockSpec((tm, bw), lambda i, j: (i, 0))
    return pl.pallas_call(
        _merge_kernel,
        grid=(m // tm, d // tn),
        in_specs=[
            br, br, br,
            pl.BlockSpec((N_BRANCH, bw, tn), lambda i, j: (0, 0, j)),
            pl.BlockSpec((tm, tn), lambda i, j: (i, g0 + j)),
            pl.BlockSpec((tm, tn), lambda i, j: (i, g0 + gstep + j)),
            pl.BlockSpec((tm, tn), lambda i, j: (i, g0 + 2 * gstep + j)),
        ],
        out_specs=pl.BlockSpec((tm, tn), lambda i, j: (i, j)),
        out_shape=jax.ShapeDtypeStruct((m, d), BF16),
        compiler_params=_params("parallel", "arbitrary"),
        name="merge",
    )(ya, yb, yc, w_branch_l, p, p, p)


def _out_kernel(a_ref, w_ref, x_ref, mod_ref, o_ref):
    o_ref[...] = x_ref[...] + mod_ref[2:3, :] * _dot(a_ref[...], w_ref[...])


def _out_proj(mixed, w_out_l, x, mod_l, m_ctx, t_lat, ctx_row, tm, tn):
    m, d = x.shape
    return pl.pallas_call(
        _out_kernel,
        grid=(m // tm, d // tn),
        in_specs=[
            pl.BlockSpec((tm, d), lambda i, j: (i, 0)),
            pl.BlockSpec((d, tn), lambda i, j: (0, j)),
            pl.BlockSpec((tm, tn), lambda i, j: (i, j)),
            pl.BlockSpec((None, 3, tn), lambda i, j: (_mod_row(i, tm, m_ctx, t_lat, ctx_row), 0, j)),
        ],
        out_specs=pl.BlockSpec((tm, tn), lambda i, j: (i, j)),
        out_shape=jax.ShapeDtypeStruct((m, d), F32),
        compiler_params=_params("parallel", "arbitrary"),
        name="out_proj",
    )(mixed, w_out_l, x, mod_l)


def _final_kernel(x_ref, g_ref, o_ref):
    x = x_ref[...]
    o_ref[...] = x * lax.rsqrt(jnp.mean(x * x, axis=-1, keepdims=True) + EPS) * g_ref[...]


def _final_norm(x, g, row0, n_rows, tile):
    d = x.shape[1]
    blk0 = row0 // tile
    return pl.pallas_call(
        _final_kernel,
        grid=(n_rows // tile,),
        in_specs=[pl.BlockSpec((tile, d), lambda i: (blk0 + i, 0)), pl.BlockSpec((1, d), lambda i: (0, 0))],
        out_specs=pl.BlockSpec((tile, d), lambda i: (i, 0)),
        out_shape=jax.ShapeDtypeStruct((n_rows, d), F32),
        compiler_params=_params("parallel"),
        name="final_norm",
    )(x, g.reshape(1, d))


def _axial_rope_tables(n_tokens):
    rows = n_tokens // GRID_W
    row = jnp.repeat(jnp.arange(rows), GRID_W).astype(F32)
    col = jnp.tile(jnp.arange(GRID_W), rows).astype(F32)
    n_freq = HEAD_DIM // 4
    inv = ROPE_THETA ** (-jnp.arange(n_freq, dtype=F32) / n_freq)
    ang = jnp.concatenate([row[:, None] * inv, col[:, None] * inv], axis=-1)
    cos, sin = jnp.cos(ang), jnp.sin(ang)
    return jnp.concatenate([cos, cos], axis=-1), jnp.concatenate([-sin, sin], axis=-1)


def kernel(x_prompt, x_sample, state_delta, cache_win_k, cache_win_v, cache_glob_k, cache_glob_v, c, c_ctx,
           norm_g, w_mod, b_mod, w_in, conv_w, a_log, dt_bias, norm_a, sink, q_norm, k_norm, w_branch, w_out,
           final_g):
    n_layers, d = norm_g.shape
    bw = d // 2
    heads = bw // HEAD_DIM
    nbc, s_len, _ = x_prompt.shape
    nbl, t_len, _ = x_sample.shape
    past = cache_win_k.shape[2]
    m_ctx, m_lat = nbc * s_len, nbl * t_len
    m = m_ctx + m_lat
    assert m_ctx % t_len == 0 and t_len % s_len == 0 and s_len % (2 * WINDOW) == 0 and 8 * heads <= LANES
    assert heads % KV_HEADS == 0 and heads % QUAD == 0 and bw % KV_W == 0

    widths = dict(a_qkv=3 * bw, a_z=bw, b_q=bw, b_k=KV_W, b_v=KV_W, b_z=bw, c_q=bw, c_k=KV_W, c_v=KV_W, c_z=bw,
                  gates=N_BRANCH * d)
    off, col = {}, 0
    for name, wd in widths.items():
        off[name] = col
        col += wd
    n_main = col
    n_ba = 4 * heads
    src = 4 * bw

    tile_seq = s_len
    tm_big = _pick_tile(math.gcd(m_ctx, t_len), (1024, 512, 256))
    tm_merge = _pick_tile(m, (1024, 512))
    tn = _pick_tile(math.gcd(d, KV_W), (512,))
    tn_wide = _pick_tile(math.gcd(d, n_main), (1024, tn))
    tb = 4 * CHUNK

    w_in16 = w_in.astype(BF16)
    w_main = jnp.concatenate([w_in16[:, :, :src], w_in16[:, :, src + n_ba:]], axis=-1)
    w_ba = jnp.pad(w_in16[:, :, src:src + n_ba], ((0, 0), (0, 0), (0, LANES - n_ba)))
    w_branch16 = w_branch.astype(BF16)
    w_out16 = w_out.astype(BF16)

    ctx_row = nbl
    mod_rows = -(-(nbl + 1) // 8) * 8
    c_all = jnp.concatenate([c, c_ctx[None], jnp.zeros((mod_rows - nbl - 1, d), F32)], axis=0)
    mod = _modulation(c_all, w_mod, b_mod).reshape(n_layers, mod_rows, 3, d)

    rope = _axial_rope_tables(t_len)
    ckw = cache_win_k.reshape(nbl, n_layers, past, KV_W)
    cvw = cache_win_v.reshape(nbl, n_layers, past, KV_W)
    ckg = cache_glob_k.reshape(nbl, n_layers, past, KV_W)
    cvg = cache_glob_v.reshape(nbl, n_layers, past, KV_W)

    x = jnp.concatenate([x_prompt.reshape(m_ctx, d), x_sample.reshape(m_lat, d)], axis=0)
    states, win_k, win_v, glob_k, glob_v = [], [], [], [], []
    for l in range(n_layers):
        h = _prep(x, mod[l], norm_g[l], m_ctx, t_len, ctx_row, tile_seq)
        p = _matmul(h, w_main[l], tm_big, tn_wide, F32, "in_proj")
        ba = _matmul(h, w_ba[l], tm_big, LANES, F32, "in_proj_gates")

        qkv = _delta_conv(p, conv_w[l], s_len, nbc, 0, None, bw)
        qkv = _delta_conv(p, conv_w[l], t_len, nbl, m_ctx // t_len, qkv, bw)
        gb = _delta_gates(ba, a_log[l], dt_bias[l], heads, tile_seq)
        gq = jnp.swapaxes(gb.reshape(m // CHUNK, CHUNK, LANES)[:, :, 2 * heads:4 * heads], 1, 2)
        gq = gq.reshape(m // CHUNK, 2 * heads // QUAD, QW)
        common = dict(heads=heads, tb=tb)
        fin = dict(p=p, z_block=off["a_z"] // bw, norm_a_l=norm_a[l])
        o_f, s_f = _delta_scan(qkv, gb, gq, seq_len=s_len, n_seq=nbc, row0=0, reverse=False,
                               emit_state=True, **common)
        o_f, _ = _delta_scan(qkv, gb, gq, seq_len=t_len, n_seq=nbl, row0=m_ctx, reverse=False, out=o_f,
                             s0=state_delta, s0_index=(l, 0), **common)
        ya, s_b = _delta_scan(qkv, gb, gq, seq_len=s_len, n_seq=nbc, row0=0, reverse=True,
                              o_prev=o_f, emit_state=True, **common, **fin)
        ya, _ = _delta_scan(qkv, gb, gq, seq_len=t_len, n_seq=nbl, row0=m_ctx, reverse=True, out=ya,
                            s0=state_delta, s0_index=(l, 1), o_prev=o_f, **common, **fin)
        states.append(jnp.stack([s_f, s_b], axis=1))

        gw = (heads // KV_HEADS) * HEAD_DIM
        blocks_b = dict(q_block=off["b_q"] // gw, k_block=off["b_k"] // HEAD_DIM, v_block=off["b_v"] // HEAD_DIM,
                        z_block=off["b_z"] // gw)
        blocks_c = dict(q_block=off["c_q"] // gw, k_block=off["c_k"] // HEAD_DIM, v_block=off["c_v"] // HEAD_DIM,
                        z_block=off["c_z"] // gw)
        ctx_args = dict(seq_len=s_len, n_seq=nbc, row0=0, tq=s_len, heads=heads, band=False, out=None)
        lat_args = dict(seq_len=t_len, n_seq=nbl, row0=m_ctx, tq=WINDOW, heads=heads, layer=l, rope=rope)
        yb, _ = _attention(p, sink_l=sink[l], **blocks_b, **ctx_args)
        yb, _ = _attention(p, out=yb, sink_l=sink[l], band=True, ctx_k=ckw, ctx_v=cvw, **blocks_b, **lat_args)
        yc, kn = _attention(p, q_norm_l=q_norm[l], k_norm_l=k_norm[l], emit_k=True, **blocks_c, **ctx_args)
        yc, _ = _attention(p, out=yc, q_norm_l=q_norm[l], k_norm_l=k_norm[l], band=False, ctx_k=ckg, ctx_v=cvg,
                           **blocks_c, **lat_args)

        def ctx_slab(name):
            return p[:m_ctx, off[name]:off[name] + KV_W].reshape(nbc, s_len, KV_HEADS, HEAD_DIM)

        win_k.append(ctx_slab("b_k"))
        win_v.append(ctx_slab("b_v"))
        glob_k.append(kn.reshape(nbc, s_len, KV_HEADS, HEAD_DIM))
        glob_v.append(ctx_slab("c_v"))

        mixed = _merge(ya, yb, yc, w_branch16[l], p, off["gates"], tm_merge, tn)
        x = _out_proj(mixed, w_out16[l], x, mod[l], m_ctx, t_len, ctx_row, tm_big, tn_wide)

    y_prompt = _final_norm(x, final_g, 0, m_ctx, tile_seq).reshape(nbc, s_len, d)
    y_sample = _final_norm(x, final_g, m_ctx, m_lat, tile_seq).reshape(nbl, t_len, d)
    return (y_prompt, y_sample, jnp.stack(states, axis=1), jnp.stack(win_k, axis=1), jnp.stack(win_v, axis=1),
            jnp.stack(glob_k, axis=1), jnp.stack(glob_v, axis=1))
```

```python
import functools
import math

import jax
import jax.numpy as jnp
import numpy as np
from jax import lax
from jax.experimental import pallas as pl
from jax.experimental.pallas import tpu as pltpu

HEAD_DIM = 128
KV_HEADS = 4
KV_W = KV_HEADS * HEAD_DIM
CHUNK = 64
WINDOW = 128
GRID_W = 64
CONV_K = 3
N_BRANCH = 3
ROPE_THETA = 10000.0
EPS = 1e-6
NEG = -1e30
LOG2E = math.log2(math.e)
KEY_CHUNK = 512

LANES = 128
VMEM_LIMIT_BYTES = 56 * 1024 * 1024

F32 = jnp.float32
BF16 = jnp.bfloat16


def _params(*sem):
    return pltpu.CompilerParams(dimension_semantics=sem, vmem_limit_bytes=VMEM_LIMIT_BYTES)


def _dot(a, b):
    return jnp.dot(a, b, preferred_element_type=F32)


def _dot_nt(a, b):
    return lax.dot_general(a, b, (((1,), (1,)), ((), ())), preferred_element_type=F32)


def _dot_tn(a, b):
    return lax.dot_general(a, b, (((0,), (0,)), ((), ())), preferred_element_type=F32)


def _silu(x):
    return x * jax.nn.sigmoid(x)


def _with_alias(in_specs, args, out):
    if out is None:
        return in_specs, args, {}
    return in_specs + [pl.BlockSpec(memory_space=pl.ANY)], args + [out], {len(args): 0}


def _pick_tile(n, candidates):
    for c in candidates:
        if n % c == 0:
            return c
    raise ValueError(f"no tile in {candidates} divides {n}")


MOD_K_TILE = 256


def _mod_kernel(c_ref, w_ref, b_ref, o_ref):
    @pl.when(pl.program_id(1) == 0)
    def _():
        o_ref[...] = jnp.broadcast_to(b_ref[...], o_ref.shape)

    a = _silu(c_ref[...]).astype(BF16)
    o_ref[...] += _dot(a, w_ref[...].astype(BF16))


def _modulation(c_all, w_mod, b_mod):
    n_layers, d, n3 = w_mod.shape
    rows = c_all.shape[0]
    kt = _pick_tile(d, (MOD_K_TILE, LANES))
    return pl.pallas_call(
        _mod_kernel,
        grid=(n_layers, d // kt),
        in_specs=[
            pl.BlockSpec((rows, kt), lambda l, k: (0, k)),
            pl.BlockSpec((None, kt, n3), lambda l, k: (l, k, 0)),
            pl.BlockSpec((None, 1, n3), lambda l, k: (l, 0, 0)),
        ],
        out_specs=pl.BlockSpec((None, rows, n3), lambda l, k: (l, 0, 0)),
        out_shape=jax.ShapeDtypeStruct((n_layers, rows, n3), F32),
        compiler_params=_params("parallel", "arbitrary"),
        name="modulation",
    )(c_all, w_mod, b_mod.reshape(n_layers, 1, n3))


def _prep_kernel(x_ref, mod_ref, g_ref, o_ref):
    x = x_ref[...]
    y = x * lax.rsqrt(jnp.mean(x * x, axis=-1, keepdims=True) + EPS) * g_ref[...]
    shift = mod_ref[0:1, :]
    scale = mod_ref[1:2, :]
    o_ref[...] = (y * (1.0 + scale) + shift).astype(o_ref.dtype)


def _mod_row(i, tile, m_ctx, t_lat, ctx_row):
    start = i * tile
    return jnp.where(start < m_ctx, ctx_row, (start - m_ctx) // t_lat)


def _prep(x, mod_l, norm_g_l, m_ctx, t_lat, ctx_row, tile):
    m, d = x.shape
    return pl.pallas_call(
        _prep_kernel,
        grid=(m // tile,),
        in_specs=[
            pl.BlockSpec((tile, d), lambda i: (i, 0)),
            pl.BlockSpec((None, 3, d), lambda i: (_mod_row(i, tile, m_ctx, t_lat, ctx_row), 0, 0)),
            pl.BlockSpec((1, d), lambda i: (0, 0)),
        ],
        out_specs=pl.BlockSpec((tile, d), lambda i: (i, 0)),
        out_shape=jax.ShapeDtypeStruct((m, d), BF16),
        compiler_params=_params("parallel"),
        name="prep",
    )(x, mod_l, norm_g_l.reshape(1, d))


def _mm_kernel(a_ref, w_ref, o_ref):
    o_ref[...] = _dot(a_ref[...], w_ref[...]).astype(o_ref.dtype)


def _matmul(a, w, tm, tn, out_dtype, name):
    m, k = a.shape
    _, n = w.shape
    return pl.pallas_call(
        _mm_kernel,
        grid=(m // tm, n // tn),
        in_specs=[
            pl.BlockSpec((tm, k), lambda i, j: (i, 0)),
            pl.BlockSpec((k, tn), lambda i, j: (0, j)),
        ],
        out_specs=pl.BlockSpec((tm, tn), lambda i, j: (i, j)),
        out_shape=jax.ShapeDtypeStruct((m, n), out_dtype),
        compiler_params=_params("parallel", "arbitrary"),
        name=name,
    )(a, w)


def _conv_kernel(x_ref, w_ref, *rest, n_qk_blocks):
    o_ref = rest[-1]
    x = x_ref[...]
    t = x.shape[0]
    row = lax.broadcasted_iota(jnp.int32, x.shape, 0)
    prev = jnp.where(row == 0, 0.0, pltpu.roll(x, 1, 0))
    nxt = jnp.where(row == t - 1, 0.0, pltpu.roll(x, t - 1, 0))
    y = _silu(prev * w_ref[0:1, :] + x * w_ref[1:2, :] + nxt * w_ref[2:3, :])
    is_qk = pl.program_id(1) < n_qk_blocks
    for s in range(x.shape[1] // HEAD_DIM):
        ys = y[:, s * HEAD_DIM:(s + 1) * HEAD_DIM]
        inv = lax.rsqrt(jnp.sum(ys * ys, axis=-1, keepdims=True) + EPS)
        o_ref[:, s * HEAD_DIM:(s + 1) * HEAD_DIM] = ys * jnp.where(is_qk, inv, 1.0)


CONV_BLOCK_BYTES = 4 * 1024 * 1024


def _delta_conv(p, conv_w_l, seq_len, n_seq, row_block0, out, bw):
    cols = HEAD_DIM
    while bw % (2 * cols) == 0 and seq_len * 2 * cols * 4 <= CONV_BLOCK_BYTES:
        cols *= 2
    n_col = 3 * bw // cols
    kern = functools.partial(_conv_kernel, n_qk_blocks=2 * bw // cols)
    m = p.shape[0]
    in_specs = [
        pl.BlockSpec((seq_len, cols), lambda b, j: (row_block0 + b, j)),
        pl.BlockSpec((CONV_K, cols), lambda b, j: (0, j)),
    ]
    args = [p, conv_w_l]
    in_specs, args, aliases = _with_alias(in_specs, args, out)
    return pl.pallas_call(
        kern,
        grid=(n_seq, n_col),
        in_specs=in_specs,
        out_specs=pl.BlockSpec((seq_len, cols), lambda b, j: (row_block0 + b, j)),
        out_shape=jax.ShapeDtypeStruct((m, 3 * bw), F32),
        input_output_aliases=aliases,
        compiler_params=_params("parallel", "parallel"),
        name="delta_conv",
    )(*args)


def _gates_kernel(ba_ref, alog_ref, dtb_ref, o_ref, *, heads):
    ba = ba_ref[...]
    rows = ba.shape[0]
    lane = lax.broadcasted_iota(jnp.int32, ba.shape, 1)
    pos = lax.broadcasted_iota(jnp.int32, ba.shape, 0) & (CHUNK - 1)
    beta = jax.nn.sigmoid(ba)
    z = ba + dtb_ref[...]
    softplus = jnp.maximum(z, 0.0) + jnp.log1p(jnp.exp(-jnp.abs(z)))
    g = -jnp.exp(alog_ref[...]) * softplus
    pre = g
    suf = g
    step = 1
    while step < CHUNK:
        pre = pre + jnp.where(pos >= step, pltpu.roll(pre, step, 0), 0.0)
        suf = suf + jnp.where(pos < CHUNK - step, pltpu.roll(suf, rows - step, 0), 0.0)
        step *= 2
    gc = jnp.where(lane < 3 * heads, pre, suf)
    total = pre + suf - g
    eg = jnp.exp(pltpu.roll(gc, 2 * heads, 1))
    ekd = jnp.exp(pltpu.roll(total - gc, 4 * heads, 1))
    o_ref[...] = jnp.where(lane < 2 * heads, beta,
                           jnp.where(lane < 4 * heads, gc, jnp.where(lane < 6 * heads, eg, ekd)))


def _delta_gates(ba, a_log_l, dt_bias_l, heads, tile):
    m = ba.shape[0]
    pad = jnp.zeros((LANES - 4 * heads,), F32)
    lead = jnp.zeros((2 * heads,), F32)
    alog = jnp.concatenate([lead, a_log_l.reshape(-1), pad]).reshape(1, LANES)
    dtb = jnp.concatenate([lead, dt_bias_l.reshape(-1), pad]).reshape(1, LANES)
    return pl.pallas_call(
        functools.partial(_gates_kernel, heads=heads),
        grid=(m // tile,),
        in_specs=[
            pl.BlockSpec((tile, LANES), lambda i: (i, 0)),
            pl.BlockSpec((1, LANES), lambda i: (0, 0)),
            pl.BlockSpec((1, LANES), lambda i: (0, 0)),
        ],
        out_specs=pl.BlockSpec((tile, LANES), lambda i: (i, 0)),
        out_shape=jax.ShapeDtypeStruct((m, LANES), F32),
        compiler_params=_params("parallel"),
        name="delta_gates",
    )(ba, alog, dtb)


QUAD = 4
QW = QUAD * CHUNK
QD = QUAD * HEAD_DIM


def _split_bf16(x):
    hi = x.astype(BF16)
    lo = (x - hi.astype(F32)).astype(BF16)
    return hi, lo


def _tile_rows(x):
    return jnp.concatenate([x] * QUAD, axis=0)


def _quad_inverse_step(t, p, bd_mask, first, last):
    p_hi, p_lo = _split_bf16(p)
    if first:
        x_hi, x_lo = p_hi, p_lo
    else:
        t_hi, t_lo = _split_bf16(t)
        x_hi = t_hi if last else jnp.concatenate([t_hi, p_hi], axis=0)
        x_lo = t_lo if last else jnp.concatenate([t_lo, p_lo], axis=0)
    bd_hi = _tile_rows(p_hi) * bd_mask
    bd_lo = _tile_rows(p_lo) * bd_mask
    lhs = jnp.concatenate([x_hi, x_lo, x_hi], axis=1)
    rhs = jnp.concatenate([bd_hi, bd_hi, bd_lo], axis=0)
    y = _dot(lhs, rhs)
    if first:
        return t + p, y
    return t + y[:CHUNK], (None if last else y[CHUNK:])


def _delta_kernel(*refs, heads, reverse, has_s0, finalize, emit_state, has_alias):
    it = iter(refs)
    q_ref, k_ref, v_ref, gb_ref, gq_ref = (next(it) for _ in range(5))
    s0_ref = next(it) if has_s0 else None
    if finalize:
        oprev_ref, z_ref, na_ref = (next(it) for _ in range(3))
    if has_alias:
        next(it)
    o_ref = next(it)
    sout_ref = next(it) if emit_state else None
    s_ref = next(it)

    n = pl.program_id(1)
    n_chunks = q_ref.shape[0] // CHUNK
    n_quads = heads // QUAD
    direction = 1 if reverse else 0
    quads = range(n_quads)

    @pl.when(n == 0)
    def _():
        if has_s0:
            s_ref[...] = s0_ref[...]
        else:
            s_ref[...] = jnp.zeros_like(s_ref)

    ii = lax.broadcasted_iota(jnp.int32, (CHUNK, QW), 0)
    jq = lax.broadcasted_iota(jnp.int32, (CHUNK, QW), 1)
    jj = jq & (CHUNK - 1)
    if reverse:
        causal, strict = ii <= jj, ii < jj
    else:
        causal, strict = ii >= jj, ii > jj
    eye = jnp.where(ii == jj, 1.0, 0.0)
    br = lax.broadcasted_iota(jnp.int32, (QW, QW), 0) // CHUNK
    bd_mask = jnp.where(br == lax.broadcasted_iota(jnp.int32, (QW, QW), 1) // CHUNK, 1.0, 0.0).astype(BF16)
    kr = lax.broadcasted_iota(jnp.int32, (QW, QD), 0) // CHUNK
    k_mask = jnp.where(kr == lax.broadcasted_iota(jnp.int32, (QW, QD), 1) // HEAD_DIM, 1.0, 0.0).astype(BF16)
    zeros_s = jnp.zeros((HEAD_DIM, HEAD_DIM), BF16)
    scale = HEAD_DIM ** -0.5

    def chunk_body(ci, carry):
        c = (n_chunks - 1 - ci) if reverse else ci
        r0 = pl.multiple_of(c * CHUNK, CHUNK)
        rows = pl.ds(r0, CHUNK)
        gb = gb_ref[rows, :]
        gq = gq_ref[c]
        edge = 0 if reverse else CHUNK - 1

        def per_head(group, qd):
            l0 = group * 2 * heads + direction * heads + qd * QUAD
            return jnp.concatenate(
                [jnp.broadcast_to(gb[:, l0 + j:l0 + j + 1], (CHUNK, HEAD_DIM)) for j in range(QUAD)], axis=1)

        def per_head_narrow(group, qd):
            l0 = group * 2 * heads + direction * heads + qd * QUAD
            out = jnp.broadcast_to(gb[:, l0 + QUAD - 1:l0 + QUAD], (CHUNK, QW))
            for j in range(QUAD - 2, -1, -1):
                out = jnp.where(jq < (j + 1) * CHUNK, jnp.broadcast_to(gb[:, l0 + j:l0 + j + 1], (CHUNK, QW)), out)
            return out

        cols = [slice(qd * QD, (qd + 1) * QD) for qd in quads]
        k = [k_ref[rows, cols[qd]] for qd in quads]
        beta = [per_head(0, qd) for qd in quads]
        eg = [per_head(2, qd) for qd in quads]
        kb = [k[qd] * beta[qd] for qd in quads]
        k16 = [k[qd].astype(BF16) for qd in quads]
        qs = [q_ref[rows, cols[qd]] * scale for qd in quads]

        kq = [_dot_nt(jnp.concatenate([kb[qd].astype(BF16), qs[qd].astype(BF16)], axis=0),
                      _tile_rows(k16[qd]) * k_mask) for qd in quads]
        decay = []
        for qd in quads:
            grow = gq[direction * n_quads + qd:direction * n_quads + qd + 1, :]
            decay.append(jnp.where(causal, jnp.exp(jnp.minimum(per_head_narrow(1, qd) - grow, 0.0)), 0.0))
        qk16 = [(kq[qd][CHUNK:] * decay[qd]).astype(BF16) for qd in quads]

        t = [eye for _ in quads]
        p = [-jnp.where(strict, kq[qd][:CHUNK] * decay[qd], 0.0) for qd in quads]
        n_steps = CHUNK.bit_length() - 1
        for step in range(n_steps):
            res = [_quad_inverse_step(t[qd], p[qd], bd_mask, step == 0, step == n_steps - 1) for qd in quads]
            t = [r[0] for r in res]
            p = [r[1] for r in res]

        uw = []
        for qd in quads:
            vb16 = (v_ref[rows, cols[qd]] * beta[qd]).astype(BF16)
            kbe16 = (kb[qd] * eg[qd]).astype(BF16)
            rhs = jnp.concatenate(
                [jnp.concatenate([vb16[:, j * HEAD_DIM:(j + 1) * HEAD_DIM],
                                  kbe16[:, j * HEAD_DIM:(j + 1) * HEAD_DIM]], axis=1) for j in range(QUAD)], axis=0)
            uw.append(_dot(_tile_rows(t[qd].astype(BF16)) * bd_mask, rhs))

        s_old = [s_ref[h] for h in range(heads)]
        v_new = [None] * heads
        o_inter = [None] * heads
        for qd in quads:
            qd16 = (qs[qd] * eg[qd]).astype(BF16)
            for pr in range(QUAD // 2):
                j0, j1 = 2 * pr, 2 * pr + 1
                h0, h1 = qd * QUAD + j0, qd * QUAD + j1
                w16 = jnp.concatenate([uw[qd][j0 * CHUNK:(j0 + 1) * CHUNK, HEAD_DIM:],
                                       uw[qd][j1 * CHUNK:(j1 + 1) * CHUNK, HEAD_DIM:]], axis=1).astype(BF16)
                lhs = jnp.concatenate([w16, qd16[:, j0 * HEAD_DIM:(j1 + 1) * HEAD_DIM]], axis=0)
                bd_s = jnp.concatenate(
                    [jnp.concatenate([s_old[h0].astype(BF16), zeros_s], axis=1),
                     jnp.concatenate([zeros_s, s_old[h1].astype(BF16)], axis=1)], axis=0)
                ws = _dot(lhs, bd_s)
                for j, h in ((j0, h0), (j1, h1)):
                    lanes = slice((j - j0) * HEAD_DIM, (j - j0 + 1) * HEAD_DIM)
                    v_new[h] = uw[qd][j * CHUNK:(j + 1) * CHUNK, :HEAD_DIM] - ws[:CHUNK, lanes]
                    o_inter[h] = ws[CHUNK:, lanes]
        v16 = [jnp.concatenate([v_new[qd * QUAD + j].astype(BF16) for j in range(QUAD)], axis=0) for qd in quads]
        o_intra = [_dot(_tile_rows(qk16[qd]) * bd_mask, v16[qd]) for qd in quads]

        s_upd = []
        for qd in quads:
            kd16 = (k[qd] * per_head(3, qd)).astype(BF16)
            kd_rows = jnp.concatenate([kd16[:, j * HEAD_DIM:(j + 1) * HEAD_DIM] for j in range(QUAD)], axis=0)
            s_upd.append(_dot_tn(kd_rows, jnp.concatenate([v16[qd]] * QUAD, axis=1) * k_mask))

        for qd in quads:
            for j in range(QUAD):
                h = qd * QUAD + j
                hc = slice(h * HEAD_DIM, (h + 1) * HEAD_DIM)
                lg = 4 * heads + direction * heads + h
                g_tot = gb[edge:edge + 1, lg:lg + 1]
                s_ref[h] = s_old[h] * g_tot + s_upd[qd][:, j * HEAD_DIM:(j + 1) * HEAD_DIM]
                o = o_inter[h] + o_intra[qd][j * CHUNK:(j + 1) * CHUNK]
                if finalize:
                    o = o + oprev_ref[rows, hc]
                    y = o * lax.rsqrt(jnp.mean(o * o, axis=-1, keepdims=True) + EPS) * na_ref[...]
                    o_ref[rows, hc] = (y * _silu(z_ref[rows, hc])).astype(o_ref.dtype)
                else:
                    o_ref[rows, hc] = o
        return carry

    lax.fori_loop(0, n_chunks, chunk_body, 0, unroll=2)

    if emit_state:
        @pl.when(n == pl.num_programs(1) - 1)
        def _():
            sout_ref[...] = s_ref[...]


def _delta_scan(qkv, gb, gq, *, heads, seq_len, n_seq, row0, tb, reverse, s0=None, s0_index=None,
                o_prev=None, p=None, z_block=None, norm_a_l=None, out=None, emit_state=False):
    bw = heads * HEAD_DIM
    m = qkv.shape[0]
    ntb = seq_len // tb
    blk0 = row0 // tb
    finalize = o_prev is not None

    def tok(b, n):
        step = (ntb - 1 - n) if reverse else n
        return blk0 + b * ntb + step

    in_specs = [
        pl.BlockSpec((tb, bw), lambda b, n: (tok(b, n), 0)),
        pl.BlockSpec((tb, bw), lambda b, n: (tok(b, n), 1)),
        pl.BlockSpec((tb, bw), lambda b, n: (tok(b, n), 2)),
        pl.BlockSpec((tb, LANES), lambda b, n: (tok(b, n), 0)),
        pl.BlockSpec((tb // CHUNK, 2 * heads // QUAD, QW), lambda b, n: (tok(b, n), 0, 0)),
    ]
    args = [qkv, qkv, qkv, gb, gq]
    if s0 is not None:
        layer, direction = s0_index
        in_specs.append(pl.BlockSpec((None, None, None, heads, HEAD_DIM, HEAD_DIM),
                                     lambda b, n: (b, layer, direction, 0, 0, 0)))
        args.append(s0)
    if finalize:
        in_specs += [
            pl.BlockSpec((tb, bw), lambda b, n: (tok(b, n), 0)),
            pl.BlockSpec((tb, bw), lambda b, n: (tok(b, n), z_block)),
            pl.BlockSpec((1, HEAD_DIM), lambda b, n: (0, 0)),
        ]
        args += [o_prev, p, norm_a_l.reshape(1, HEAD_DIM)]
    in_specs, args, aliases = _with_alias(in_specs, args, out)
    out_specs = [pl.BlockSpec((tb, bw), lambda b, n: (tok(b, n), 0))]
    out_shape = [jax.ShapeDtypeStruct((m, bw), BF16 if finalize else F32)]
    if emit_state:
        out_specs.append(pl.BlockSpec((None, heads, HEAD_DIM, HEAD_DIM), lambda b, n: (b, 0, 0, 0)))
        out_shape.append(jax.ShapeDtypeStruct((n_seq, heads, HEAD_DIM, HEAD_DIM), F32))
    kern = functools.partial(_delta_kernel, heads=heads, reverse=reverse, has_s0=s0 is not None,
                             finalize=finalize, emit_state=emit_state, has_alias=out is not None)
    res = pl.pallas_call(
        kern,
        grid=(n_seq, ntb),
        in_specs=in_specs,
        out_specs=out_specs,
        out_shape=out_shape,
        scratch_shapes=[pltpu.VMEM((heads, HEAD_DIM, HEAD_DIM), F32)],
        input_output_aliases=aliases,
        compiler_params=_params("parallel", "arbitrary"),
        name="delta_scan_bwd" if reverse else "delta_scan_fwd",
    )(*args)
    return res if emit_state else (res[0], None)


def _rope(x, cosf, sins):
    return x * cosf + pltpu.roll(x, HEAD_DIM // 2, 1) * sins


def _rms_head(x, g):
    return x * lax.rsqrt(jnp.mean(x * x, axis=-1, keepdims=True) + EPS) * g


def _attn_kernel(*refs, band, has_ctx, use_sink, use_norm, use_rope, emit_k, has_alias, t_self, s_ctx, tq, n_sub,
                 grp, key_chunk):
    it = iter(refs)
    sink_ref = next(it) if use_sink else None
    q_ref, k_ref, v_ref, z_ref = (next(it) for _ in range(4))
    if has_ctx:
        kc_ref, vc_ref = next(it), next(it)
    if use_norm:
        qn_ref, kn_ref = next(it), next(it)
    if use_rope:
        cq_ref, sq_ref, ck_ref, sk_ref = (next(it) for _ in range(4))
    if has_alias:
        next(it)
    y_ref = next(it)
    kout_ref = next(it) if emit_k else None
    ks_ref, vs_ref = next(it), next(it)

    h = pl.program_id(1)
    i = pl.program_id(2)
    pad = WINDOW if band else 0
    ctx0 = t_self + 2 * pad

    @pl.when(i == 0)
    def _():
        k = k_ref[...]
        if use_norm:
            k = _rms_head(k, kn_ref[...])
        if emit_k:
            kout_ref[...] = k
        if use_rope:
            k = _rope(k, ck_ref[...], sk_ref[...])
        def ones_lane(n_rows):
            return jnp.where(lax.broadcasted_iota(jnp.int32, (n_rows, HEAD_DIM), 1) == 0, 1.0, 0.0).astype(BF16)

        ks_ref[pad:pad + t_self, :] = k.astype(BF16)
        vs_ref[pad:pad + t_self, :HEAD_DIM] = v_ref[...].astype(BF16)
        vs_ref[pad:pad + t_self, HEAD_DIM:] = ones_lane(t_self)
        if band:
            ks_ref[0:pad, :] = jnp.zeros((pad, HEAD_DIM), BF16)
            vs_ref[0:pad, :] = jnp.zeros((pad, 2 * HEAD_DIM), BF16)
            ks_ref[pad + t_self:ctx0, :] = jnp.zeros((pad, HEAD_DIM), BF16)
            vs_ref[pad + t_self:ctx0, :] = jnp.zeros((pad, 2 * HEAD_DIM), BF16)
        if has_ctx:
            ks_ref[ctx0:ctx0 + s_ctx, :] = kc_ref[...].astype(BF16)
            vs_ref[ctx0:ctx0 + s_ctx, :HEAD_DIM] = vc_ref[...].astype(BF16)
            vs_ref[ctx0:ctx0 + s_ctx, HEAD_DIM:] = ones_lane(s_ctx)

    scale = HEAD_DIM ** -0.5 * LOG2E
    rows = grp * tq
    if band:
        width = 3 * WINDOW
        krel = lax.broadcasted_iota(jnp.int32, (rows, width), 1) - WINDOW
        in_band = jnp.abs((lax.broadcasted_iota(jnp.int32, (rows, width), 0) & (tq - 1)) - krel) <= WINDOW

    def q_block(sub):
        rs = slice(sub * tq, (sub + 1) * tq)
        qi = i * n_sub + sub
        qs = []
        for g in range(grp):
            q = q_ref[rs, g * HEAD_DIM:(g + 1) * HEAD_DIM]
            if use_norm:
                q = _rms_head(q, qn_ref[...])
            if use_rope:
                q = _rope(q, cq_ref[rs, :], sq_ref[rs, :])
            qs.append((q * scale).astype(BF16))
        q16 = jnp.concatenate(qs, axis=0)

        if band:
            r0 = pl.multiple_of(qi * tq, tq)
            kpos = krel + qi * tq
            valid = in_band & (kpos >= 0) & (kpos < t_self)
            pieces = [(lambda: jnp.where(valid, _dot_nt(q16, ks_ref[pl.ds(r0, width), :]), NEG),
                       lambda: vs_ref[pl.ds(r0, width), :])]
            bounds = [(ctx0, ctx0 + s_ctx)] if has_ctx else []
        else:
            step = min(t_self, key_chunk)
            bounds = [(a, a + step) for a in range(0, t_self, step)]
            if has_ctx:
                bounds.append((t_self, t_self + s_ctx))
            pieces = []
        for a, b in bounds:
            pieces.append((lambda a=a, b=b: _dot_nt(q16, ks_ref[a:b, :]), lambda a=a, b=b: vs_ref[a:b, :]))

        if use_sink:
            rid = lax.broadcasted_iota(jnp.int32, (rows, 1), 0)
            sink = jnp.zeros((rows, 1), F32)
            for g in range(grp):
                sink = jnp.where(rid >= g * tq, sink_ref[h * grp + g] * LOG2E, sink)
            m = sink
            acc = jnp.where(lax.broadcasted_iota(jnp.int32, (rows, 2 * HEAD_DIM), 1) == HEAD_DIM, 1.0, 0.0)
        else:
            m = acc = None
        for scores, values in pieces:
            s = scores()
            m_new = jnp.max(s, axis=-1, keepdims=True)
            if m is not None:
                m_new = jnp.maximum(m, m_new)
            pv = _dot(jnp.exp2(s - m_new).astype(BF16), values())
            acc = pv if m is None else jnp.exp2(m - m_new) * acc + pv
            m = m_new
        o = acc[:, :HEAD_DIM] / acc[:, HEAD_DIM:HEAD_DIM + 1]
        for g in range(grp):
            cols = slice(g * HEAD_DIM, (g + 1) * HEAD_DIM)
            y_ref[rs, cols] = (o[g * tq:(g + 1) * tq] * _silu(z_ref[rs, cols])).astype(y_ref.dtype)

    for sub in range(n_sub):
        q_block(sub)


def _attention(p, *, q_block, k_block, v_block, z_block, seq_len, n_seq, row0, tq, heads, band, out,
               ctx_k=None, ctx_v=None, layer=None, sink_l=None, q_norm_l=None, k_norm_l=None, rope=None,
               emit_k=False):
    m = p.shape[0]
    bw = heads * HEAD_DIM
    grp = heads // KV_HEADS
    gw = grp * HEAD_DIM
    n_sub = 4
    while seq_len % (n_sub * tq):
        n_sub //= 2
    tqs = n_sub * tq
    nq = seq_len // tqs
    qblk0 = row0 // tqs
    sblk0 = row0 // seq_len
    has_ctx = ctx_k is not None
    s_ctx = ctx_k.shape[2] if has_ctx else 0
    use_sink = sink_l is not None
    use_norm = q_norm_l is not None
    use_rope = rope is not None
    pad = WINDOW if band else 0
    assert seq_len <= KEY_CHUNK or seq_len % KEY_CHUNK == 0

    in_specs, args = [], []
    if use_sink:
        in_specs.append(pl.BlockSpec(memory_space=pltpu.SMEM))
        args.append(sink_l)
    in_specs += [
        pl.BlockSpec((tqs, gw), lambda b, h, i: (qblk0 + b * nq + i, q_block + h)),
        pl.BlockSpec((seq_len, HEAD_DIM), lambda b, h, i: (sblk0 + b, k_block + h)),
        pl.BlockSpec((seq_len, HEAD_DIM), lambda b, h, i: (sblk0 + b, v_block + h)),
        pl.BlockSpec((tqs, gw), lambda b, h, i: (qblk0 + b * nq + i, z_block + h)),
    ]
    args += [p, p, p, p]
    if has_ctx:
        spec = pl.BlockSpec((None, None, s_ctx, HEAD_DIM), lambda b, h, i: (b, layer, 0, h))
        in_specs += [spec, spec]
        args += [ctx_k, ctx_v]
    if use_norm:
        spec = pl.BlockSpec((1, HEAD_DIM), lambda b, h, i: (0, 0))
        in_specs += [spec, spec]
        args += [q_norm_l.reshape(1, HEAD_DIM), k_norm_l.reshape(1, HEAD_DIM)]
    if use_rope:
        cosf, sins = rope
        qspec = pl.BlockSpec((tqs, HEAD_DIM), lambda b, h, i: (i, 0))
        kspec = pl.BlockSpec((seq_len, HEAD_DIM), lambda b, h, i: (0, 0))
        in_specs += [qspec, qspec, kspec, kspec]
        args += [cosf, sins, cosf, sins]
    in_specs, args, aliases = _with_alias(in_specs, args, out)
    out_specs = [pl.BlockSpec((tqs, gw), lambda b, h, i: (qblk0 + b * nq + i, h))]
    out_shape = [jax.ShapeDtypeStruct((m, bw), BF16)]
    if emit_k:
        out_specs.append(pl.BlockSpec((None, seq_len, HEAD_DIM), lambda b, h, i: (b, 0, h)))
        out_shape.append(jax.ShapeDtypeStruct((n_seq, seq_len, KV_W), F32))
    kern = functools.partial(_attn_kernel, band=band, has_ctx=has_ctx, use_sink=use_sink, use_norm=use_norm,
                             use_rope=use_rope, emit_k=emit_k, has_alias=out is not None, t_self=seq_len,
                             s_ctx=s_ctx, tq=tq, n_sub=n_sub, grp=grp, key_chunk=KEY_CHUNK)
    n_keys = seq_len + 2 * pad + s_ctx
    assert tq & (tq - 1) == 0 and (not band or tq == WINDOW)
    res = pl.pallas_call(
        kern,
        grid=(n_seq, KV_HEADS, nq),
        in_specs=in_specs,
        out_specs=out_specs,
        out_shape=out_shape,
        scratch_shapes=[pltpu.VMEM((n_keys, HEAD_DIM), BF16), pltpu.VMEM((n_keys, 2 * HEAD_DIM), BF16)],
        input_output_aliases=aliases,
        compiler_params=_params("parallel", "parallel", "arbitrary"),
        name=("attn_band" if band else "attn_full") + ("_ctx" if has_ctx else ""),
    )(*args)
    return res if emit_k else (res[0], None)


def _merge_kernel(ya_ref, yb_ref, yc_ref, w_ref, ga_ref, gb_ref, gc_ref, o_ref):
    acc = jax.nn.sigmoid(ga_ref[...]) * _dot(ya_ref[...], w_ref[0])
    acc = acc + jax.nn.sigmoid(gb_ref[...]) * _dot(yb_ref[...], w_ref[1])
    acc = acc + jax.nn.sigmoid(gc_ref[...]) * _dot(yc_ref[...], w_ref[2])
    o_ref[...] = acc.astype(o_ref.dtype)


def _merge(ya, yb, yc, w_branch_l, p, gate_col0, tm, tn):
    m, bw = ya.shape
    d = w_branch_l.shape[2]
    g0 = gate_col0 // tn
    gstep = d // tn
    br = pl.BlockSpec((tm, bw), lambda i, j: (i, 0))
    return pl.pallas_call(
        _merge_kernel,
        grid=(m // tm, d // tn),
        in_specs=[
            br, br, br,
            pl.BlockSpec((N_BRANCH, bw, tn), lambda i, j: (0, 0, j)),
            pl.BlockSpec((tm, tn), lambda i, j: (i, g0 + j)),
            pl.BlockSpec((tm, tn), lambda i, j: (i, g0 + gstep + j)),
            pl.BlockSpec((tm, tn), lambda i, j: (i, g0 + 2 * gstep + j)),
        ],
        out_specs=pl.BlockSpec((tm, tn), lambda i, j: (i, j)),
        out_shape=jax.ShapeDtypeStruct((m, d), BF16),
        compiler_params=_params("parallel", "arbitrary"),
        name="merge",
    )(ya, yb, yc, w_branch_l, p, p, p)


def _out_kernel(a_ref, w_ref, x_ref, mod_ref, o_ref):
    o_ref[...] = x_ref[...] + mod_ref[2:3, :] * _dot(a_ref[...], w_ref[...])


def _out_proj(mixed, w_out_l, x, mod_l, m_ctx, t_lat, ctx_row, tm, tn):
    m, d = x.shape
    return pl.pallas_call(
        _out_kernel,
        grid=(m // tm, d // tn),
        in_specs=[
            pl.BlockSpec((tm, d), lambda i, j: (i, 0)),
            pl.BlockSpec((d, tn), lambda i, j: (0, j)),
            pl.BlockSpec((tm, tn), lambda i, j: (i, j)),
            pl.BlockSpec((None, 3, tn), lambda i, j: (_mod_row(i, tm, m_ctx, t_lat, ctx_row), 0, j)),
        ],
        out_specs=pl.BlockSpec((tm, tn), lambda i, j: (i, j)),
        out_shape=jax.ShapeDtypeStruct((m, d), F32),
        compiler_params=_params("parallel", "arbitrary"),
        name="out_proj",
    )(mixed, w_out_l, x, mod_l)


def _final_kernel(x_ref, g_ref, o_ref):
    x = x_ref[...]
    o_ref[...] = x * lax.rsqrt(jnp.mean(x * x, axis=-1, keepdims=True) + EPS) * g_ref[...]


def _final_norm(x, g, row0, n_rows, tile):
    d = x.shape[1]
    blk0 = row0 // tile
    return pl.pallas_call(
        _final_kernel,
        grid=(n_rows // tile,),
        in_specs=[pl.BlockSpec((tile, d), lambda i: (blk0 + i, 0)), pl.BlockSpec((1, d), lambda i: (0, 0))],
        out_specs=pl.BlockSpec((tile, d), lambda i: (i, 0)),
        out_shape=jax.ShapeDtypeStruct((n_rows, d), F32),
        compiler_params=_params("parallel"),
        name="final_norm",
    )(x, g.reshape(1, d))


def _axial_rope_tables(n_tokens):
    rows = n_tokens // GRID_W
    row = jnp.repeat(jnp.arange(rows), GRID_W).astype(F32)
    col = jnp.tile(jnp.arange(GRID_W), rows).astype(F32)
    n_freq = HEAD_DIM // 4
    inv = ROPE_THETA ** (-jnp.arange(n_freq, dtype=F32) / n_freq)
    ang = jnp.concatenate([row[:, None] * inv, col[:, None] * inv], axis=-1)
    cos, sin = jnp.cos(ang), jnp.sin(ang)
    return jnp.concatenate([cos, cos], axis=-1), jnp.concatenate([-sin, sin], axis=-1)


def kernel(x_prompt, x_sample, state_delta, cache_win_k, cache_win_v, cache_glob_k, cache_glob_v, c, c_ctx,
           norm_g, w_mod, b_mod, w_in, conv_w, a_log, dt_bias, norm_a, sink, q_norm, k_norm, w_branch, w_out,
           final_g):
    n_layers, d = norm_g.shape
    bw = d // 2
    heads = bw // HEAD_DIM
    nbc, s_len, _ = x_prompt.shape
    nbl, t_len, _ = x_sample.shape
    past = cache_win_k.shape[2]
    m_ctx, m_lat = nbc * s_len, nbl * t_len
    m = m_ctx + m_lat
    assert m_ctx % t_len == 0 and t_len % s_len == 0 and s_len % (2 * WINDOW) == 0 and 8 * heads <= LANES
    assert heads % KV_HEADS == 0 and heads % QUAD == 0 and bw % KV_W == 0

    widths = dict(a_qkv=3 * bw, a_z=bw, b_q=bw, b_k=KV_W, b_v=KV_W, b_z=bw, c_q=bw, c_k=KV_W, c_v=KV_W, c_z=bw,
                  gates=N_BRANCH * d)
    off, col = {}, 0
    for name, wd in widths.items():
        off[name] = col
        col += wd
    n_main = col
    n_ba = 4 * heads
    src = 4 * bw

    tile_seq = s_len
    tm_big = _pick_tile(math.gcd(m_ctx, t_len), (1024, 512, 256))
    tm_merge = _pick_tile(m, (1024, 512))
    tn = _pick_tile(math.gcd(d, KV_W), (512,))
    tn_wide = _pick_tile(math.gcd(d, n_main), (1024, tn))
    tb = 4 * CHUNK

    w_in16 = w_in.astype(BF16)
    w_main = jnp.concatenate([w_in16[:, :, :src], w_in16[:, :, src + n_ba:]], axis=-1)
    w_ba = jnp.pad(w_in16[:, :, src:src + n_ba], ((0, 0), (0, 0), (0, LANES - n_ba)))
    w_branch16 = w_branch.astype(BF16)
    w_out16 = w_out.astype(BF16)

    ctx_row = nbl
    mod_rows = -(-(nbl + 1) // 8) * 8
    c_all = jnp.concatenate([c, c_ctx[None], jnp.zeros((mod_rows - nbl - 1, d), F32)], axis=0)
    mod = _modulation(c_all, w_mod, b_mod).reshape(n_layers, mod_rows, 3, d)

    rope = _axial_rope_tables(t_len)
    ckw = cache_win_k.reshape(nbl, n_layers, past, KV_W)
    cvw = cache_win_v.reshape(nbl, n_layers, past, KV_W)
    ckg = cache_glob_k.reshape(nbl, n_layers, past, KV_W)
    cvg = cache_glob_v.reshape(nbl, n_layers, past, KV_W)

    x = jnp.concatenate([x_prompt.reshape(m_ctx, d), x_sample.reshape(m_lat, d)], axis=0)
    states, win_k, win_v, glob_k, glob_v = [], [], [], [], []
    for l in range(n_layers):
        h = _prep(x, mod[l], norm_g[l], m_ctx, t_len, ctx_row, tile_seq)
        p = _matmul(h, w_main[l], tm_big, tn_wide, F32, "in_proj")
        ba = _matmul(h, w_ba[l], tm_big, LANES, F32, "in_proj_gates")

        qkv = _delta_conv(p, conv_w[l], s_len, nbc, 0, None, bw)
        qkv = _delta_conv(p, conv_w[l], t_len, nbl, m_ctx // t_len, qkv, bw)
        gb = _delta_gates(ba, a_log[l], dt_bias[l], heads, tile_seq)
        gq = jnp.swapaxes(gb.reshape(m // CHUNK, CHUNK, LANES)[:, :, 2 * heads:4 * heads], 1, 2)
        gq = gq.reshape(m // CHUNK, 2 * heads // QUAD, QW)
        common = dict(heads=heads, tb=tb)
        fin = dict(p=p, z_block=off["a_z"] // bw, norm_a_l=norm_a[l])
        o_f, s_f = _delta_scan(qkv, gb, gq, seq_len=s_len, n_seq=nbc, row0=0, reverse=False,
                               emit_state=True, **common)
        o_f, _ = _delta_scan(qkv, gb, gq, seq_len=t_len, n_seq=nbl, row0=m_ctx, reverse=False, out=o_f,
                             s0=state_delta, s0_index=(l, 0), **common)
        ya, s_b = _delta_scan(qkv, gb, gq, seq_len=s_len, n_seq=nbc, row0=0, reverse=True,
                              o_prev=o_f, emit_state=True, **common, **fin)
        ya, _ = _delta_scan(qkv, gb, gq, seq_len=t_len, n_seq=nbl, row0=m_ctx, reverse=True, out=ya,
                            s0=state_delta, s0_index=(l, 1), o_prev=o_f, **common, **fin)
        states.append(jnp.stack([s_f, s_b], axis=1))

        gw = (heads // KV_HEADS) * HEAD_DIM
        blocks_b = dict(q_block=off["b_q"] // gw, k_block=off["b_k"] // HEAD_DIM, v_block=off["b_v"] // HEAD_DIM,
                        z_block=off["b_z"] // gw)
        blocks_c = dict(q_block=off["c_q"] // gw, k_block=off["c_k"] // HEAD_DIM, v_block=off["c_v"] // HEAD_DIM,
                        z_block=off["c_z"] // gw)
        ctx_args = dict(seq_len=s_len, n_seq=nbc, row0=0, tq=s_len, heads=heads, band=False, out=None)
        lat_args = dict(seq_len=t_len, n_seq=nbl, row0=m_ctx, tq=WINDOW, heads=heads, layer=l, rope=rope)
        yb, _ = _attention(p, sink_l=sink[l], **blocks_b, **ctx_args)
        yb, _ = _attention(p, out=yb, sink_l=sink[l], band=True, ctx_k=ckw, ctx_v=cvw, **blocks_b, **lat_args)
        yc, kn = _attention(p, q_norm_l=q_norm[l], k_norm_l=k_norm[l], emit_k=True, **blocks_c, **ctx_args)
        yc, _ = _attention(p, out=yc, q_norm_l=q_norm[l], k_norm_l=k_norm[l], band=False, ctx_k=ckg, ctx_v=cvg,
                           **blocks_c, **lat_args)

        def ctx_slab(name):
            return p[:m_ctx, off[name]:off[name] + KV_W].reshape(nbc, s_len, KV_HEADS, HEAD_DIM)

        win_k.append(ctx_slab("b_k"))
        win_v.append(ctx_slab("b_v"))
        glob_k.append(kn.reshape(nbc, s_len, KV_HEADS, HEAD_DIM))
        glob_v.append(ctx_slab("c_v"))

        mixed = _merge(ya, yb, yc, w_branch16[l], p, off["gates"], tm_merge, tn)
        x = _out_proj(mixed, w_out16[l], x, mod[l], m_ctx, t_len, ctx_row, tm_big, tn_wide)

    y_prompt = _final_norm(x, final_g, 0, m_ctx, tile_seq).reshape(nbc, s_len, d)
    y_sample = _final_norm(x, final_g, m_ctx, m_lat, tile_seq).reshape(nbl, t_len, d)
    return (y_prompt, y_sample, jnp.stack(states, axis=1), jnp.stack(win_k, axis=1), jnp.stack(win_v, axis=1),
            jnp.stack(glob_k, axis=1), jnp.stack(glob_v, axis=1))
```

```python
import functools
import math

import jax
import jax.numpy as jnp
import numpy as np
from jax import lax
from jax.experimental import pallas as pl
from jax.experimental.pallas import tpu as pltpu

HEAD_DIM = 128
KV_HEADS = 4
KV_W = KV_HEADS * HEAD_DIM
CHUNK = 64
WINDOW = 128
GRID_W = 64
CONV_K = 3
N_BRANCH = 3
ROPE_THETA = 10000.0
EPS = 1e-6
NEG = -1e30
LOG2E = math.log2(math.e)
KEY_CHUNK = 512

LANES = 128
VMEM_LIMIT_BYTES = 56 * 1024 * 1024

F32 = jnp.float32
BF16 = jnp.bfloat16


def _params(*sem):
    return pltpu.CompilerParams(dimension_semantics=sem, vmem_limit_bytes=VMEM_LIMIT_BYTES)


def _dot(a, b):
    return jnp.dot(a, b, preferred_element_type=F32)


def _dot_nt(a, b):
    return lax.dot_general(a, b, (((1,), (1,)), ((), ())), preferred_element_type=F32)


def _dot_tn(a, b):
    return lax.dot_general(a, b, (((0,), (0,)), ((), ())), preferred_element_type=F32)


def _silu(x):
    return x * jax.nn.sigmoid(x)


def _with_alias(in_specs, args, out):
    if out is None:
        return in_specs, args, {}
    return in_specs + [pl.BlockSpec(memory_space=pl.ANY)], args + [out], {len(args): 0}


def _pick_tile(n, candidates):
    for c in candidates:
        if n % c == 0:
            return c
    raise ValueError(f"no tile in {candidates} divides {n}")


MOD_K_TILE = 256


def _mod_kernel(c_ref, w_ref, b_ref, o_ref):
    @pl.when(pl.program_id(1) == 0)
    def _():
        o_ref[...] = jnp.broadcast_to(b_ref[...], o_ref.shape)

    a = _silu(c_ref[...]).astype(BF16)
    o_ref[...] += _dot(a, w_ref[...].astype(BF16))


def _modulation(c_all, w_mod, b_mod):
    n_layers, d, n3 = w_mod.shape
    rows = c_all.shape[0]
    kt = _pick_tile(d, (MOD_K_TILE, LANES))
    return pl.pallas_call(
        _mod_kernel,
        grid=(n_layers, d // kt),
        in_specs=[
            pl.BlockSpec((rows, kt), lambda l, k: (0, k)),
            pl.BlockSpec((None, kt, n3), lambda l, k: (l, k, 0)),
            pl.BlockSpec((None, 1, n3), lambda l, k: (l, 0, 0)),
        ],
        out_specs=pl.BlockSpec((None, rows, n3), lambda l, k: (l, 0, 0)),
        out_shape=jax.ShapeDtypeStruct((n_layers, rows, n3), F32),
        compiler_params=_params("parallel", "arbitrary"),
        name="modulation",
    )(c_all, w_mod, b_mod.reshape(n_layers, 1, n3))


def _prep_kernel(x_ref, mod_ref, g_ref, o_ref):
    x = x_ref[...]
    y = x * lax.rsqrt(jnp.mean(x * x, axis=-1, keepdims=True) + EPS) * g_ref[...]
    shift = mod_ref[0:1, :]
    scale = mod_ref[1:2, :]
    o_ref[...] = (y * (1.0 + scale) + shift).astype(o_ref.dtype)


def _mod_row(i, tile, m_ctx, t_lat, ctx_row):
    start = i * tile
    return jnp.where(start < m_ctx, ctx_row, (start - m_ctx) // t_lat)


def _prep(x, mod_l, norm_g_l, m_ctx, t_lat, ctx_row, tile):
    m, d = x.shape
    return pl.pallas_call(
        _prep_kernel,
        grid=(m // tile,),
        in_specs=[
            pl.BlockSpec((tile, d), lambda i: (i, 0)),
            pl.BlockSpec((None, 3, d), lambda i: (_mod_row(i, tile, m_ctx, t_lat, ctx_row), 0, 0)),
            pl.BlockSpec((1, d), lambda i: (0, 0)),
        ],
        out_specs=pl.BlockSpec((tile, d), lambda i: (i, 0)),
        out_shape=jax.ShapeDtypeStruct((m, d), BF16),
        compiler_params=_params("parallel"),
        name="prep",
    )(x, mod_l, norm_g_l.reshape(1, d))


def _mm_kernel(a_ref, w_ref, o_ref):
    o_ref[...] = _dot(a_ref[...], w_ref[...]).astype(o_ref.dtype)


def _matmul(a, w, tm, tn, out_dtype, name):
    m, k = a.shape
    _, n = w.shape
    return pl.pallas_call(
        _mm_kernel,
        grid=(m // tm, n // tn),
        in_specs=[
            pl.BlockSpec((tm, k), lambda i, j: (i, 0)),
            pl.BlockSpec((k, tn), lambda i, j: (0, j)),
        ],
        out_specs=pl.BlockSpec((tm, tn), lambda i, j: (i, j)),
        out_shape=jax.ShapeDtypeStruct((m, n), out_dtype),
        compiler_params=_params("parallel", "arbitrary"),
        name=name,
    )(a, w)


def _conv_kernel(x_ref, w_ref, *rest, n_qk_blocks):
    o_ref = rest[-1]
    x = x_ref[...]
    t = x.shape[0]
    row = lax.broadcasted_iota(jnp.int32, x.shape, 0)
    prev = jnp.where(row == 0, 0.0, pltpu.roll(x, 1, 0))
    nxt = jnp.where(row == t - 1, 0.0, pltpu.roll(x, t - 1, 0))
    y = _silu(prev * w_ref[0:1, :] + x * w_ref[1:2, :] + nxt * w_ref[2:3, :])
    is_qk = pl.program_id(1) < n_qk_blocks
    for s in range(x.shape[1] // HEAD_DIM):
        ys = y[:, s * HEAD_DIM:(s + 1) * HEAD_DIM]
        inv = lax.rsqrt(jnp.sum(ys * ys, axis=-1, keepdims=True) + EPS)
        o_ref[:, s * HEAD_DIM:(s + 1) * HEAD_DIM] = ys * jnp.where(is_qk, inv, 1.0)


CONV_BLOCK_BYTES = 4 * 1024 * 1024


def _delta_conv(p, conv_w_l, seq_len, n_seq, row_block0, out, bw):
    cols = HEAD_DIM
    while bw % (2 * cols) == 0 and seq_len * 2 * cols * 4 <= CONV_BLOCK_BYTES:
        cols *= 2
    n_col = 3 * bw // cols
    kern = functools.partial(_conv_kernel, n_qk_blocks=2 * bw // cols)
    m = p.shape[0]
    in_specs = [
        pl.BlockSpec((seq_len, cols), lambda b, j: (row_block0 + b, j)),
        pl.BlockSpec((CONV_K, cols), lambda b, j: (0, j)),
    ]
    args = [p, conv_w_l]
    in_specs, args, aliases = _with_alias(in_specs, args, out)
    return pl.pallas_call(
        kern,
        grid=(n_seq, n_col),
        in_specs=in_specs,
        out_specs=pl.BlockSpec((seq_len, cols), lambda b, j: (row_block0 + b, j)),
        out_shape=jax.ShapeDtypeStruct((m, 3 * bw), F32),
        input_output_aliases=aliases,
        compiler_params=_params("parallel", "parallel"),
        name="delta_conv",
    )(*args)


def _gates_kernel(ba_ref, alog_ref, dtb_ref, o_ref, *, heads):
    ba = ba_ref[...]
    rows = ba.shape[0]
    lane = lax.broadcasted_iota(jnp.int32, ba.shape, 1)
    pos = lax.broadcasted_iota(jnp.int32, ba.shape, 0) & (CHUNK - 1)
    beta = jax.nn.sigmoid(ba)
    z = ba + dtb_ref[...]
    softplus = jnp.maximum(z, 0.0) + jnp.log1p(jnp.exp(-jnp.abs(z)))
    g = -jnp.exp(alog_ref[...]) * softplus
    pre = g
    suf = g
    step = 1
    while step < CHUNK:
        pre = pre + jnp.where(pos >= step, pltpu.roll(pre, step, 0), 0.0)
        suf = suf + jnp.where(pos < CHUNK - step, pltpu.roll(suf, rows - step, 0), 0.0)
        step *= 2
    gc = jnp.where(lane < 3 * heads, pre, suf)
    total = pre + suf - g
    eg = jnp.exp(pltpu.roll(gc, 2 * heads, 1))
    ekd = jnp.exp(pltpu.roll(total - gc, 4 * heads, 1))
    o_ref[...] = jnp.where(lane < 2 * heads, beta,
                           jnp.where(lane < 4 * heads, gc, jnp.where(lane < 6 * heads, eg, ekd)))


def _delta_gates(ba, a_log_l, dt_bias_l, heads, tile):
    m = ba.shape[0]
    pad = jnp.zeros((LANES - 4 * heads,), F32)
    lead = jnp.zeros((2 * heads,), F32)
    alog = jnp.concatenate([lead, a_log_l.reshape(-1), pad]).reshape(1, LANES)
    dtb = jnp.concatenate([lead, dt_bias_l.reshape(-1), pad]).reshape(1, LANES)
    return pl.pallas_call(
        functools.partial(_gates_kernel, heads=heads),
        grid=(m // tile,),
        in_specs=[
            pl.BlockSpec((tile, LANES), lambda i: (i, 0)),
            pl.BlockSpec((1, LANES), lambda i: (0, 0)),
            pl.BlockSpec((1, LANES), lambda i: (0, 0)),
        ],
        out_specs=pl.BlockSpec((tile, LANES), lambda i: (i, 0)),
        out_shape=jax.ShapeDtypeStruct((m, LANES), F32),
        compiler_params=_params("parallel"),
        name="delta_gates",
    )(ba, alog, dtb)


QUAD = 4
PAIR = 2
QW = QUAD * CHUNK
QD = QUAD * HEAD_DIM


def _split_bf16(x):
    hi = x.astype(BF16)
    lo = (x - hi.astype(F32)).astype(BF16)
    return hi, lo


def _tile_rows(x):
    return jnp.concatenate([x] * QUAD, axis=0)


def _quad_inverse_step(t, p, bd_mask, first, last):
    p_hi, p_lo = _split_bf16(p)
    if first:
        x_hi, x_lo = p_hi, p_lo
    else:
        t_hi, t_lo = _split_bf16(t)
        x_hi = t_hi if last else jnp.concatenate([t_hi, p_hi], axis=0)
        x_lo = t_lo if last else jnp.concatenate([t_lo, p_lo], axis=0)
    bd_hi = _tile_rows(p_hi) * bd_mask
    bd_lo = _tile_rows(p_lo) * bd_mask
    lhs = jnp.concatenate([x_hi, x_lo, x_hi], axis=1)
    rhs = jnp.concatenate([bd_hi, bd_hi, bd_lo], axis=0)
    y = _dot(lhs, rhs)
    if first:
        return t + p, y
    return t + y[:CHUNK], (None if last else y[CHUNK:])


def _delta_kernel(*refs, heads, reverse, has_s0, finalize, emit_state, has_alias):
    it = iter(refs)
    q_ref, k_ref, v_ref, gb_ref, gq_ref = (next(it) for _ in range(5))
    s0_ref = next(it) if has_s0 else None
    if finalize:
        oprev_ref, z_ref, na_ref = (next(it) for _ in range(3))
    if has_alias:
        next(it)
    o_ref = next(it)
    sout_ref = next(it) if emit_state else None
    s_ref = next(it)

    n = pl.program_id(1)
    n_chunks = q_ref.shape[0] // CHUNK
    n_quads = heads // QUAD
    direction = 1 if reverse else 0
    quads = range(n_quads)

    @pl.when(n == 0)
    def _():
        if has_s0:
            s_ref[...] = s0_ref[...]
        else:
            s_ref[...] = jnp.zeros_like(s_ref)

    ii = lax.broadcasted_iota(jnp.int32, (CHUNK, QW), 0)
    jq = lax.broadcasted_iota(jnp.int32, (CHUNK, QW), 1)
    jj = jq & (CHUNK - 1)
    if reverse:
        causal, strict = ii <= jj, ii < jj
    else:
        causal, strict = ii >= jj, ii > jj
    eye = jnp.where(ii == jj, 1.0, 0.0)
    br = lax.broadcasted_iota(jnp.int32, (QW, QW), 0) // CHUNK
    bd_mask = jnp.where(br == lax.broadcasted_iota(jnp.int32, (QW, QW), 1) // CHUNK, 1.0, 0.0).astype(BF16)
    kr = lax.broadcasted_iota(jnp.int32, (QW, QD), 0) // CHUNK
    k_mask = jnp.where(kr == lax.broadcasted_iota(jnp.int32, (QW, QD), 1) // HEAD_DIM, 1.0, 0.0).astype(BF16)
    zeros_s = jnp.zeros((HEAD_DIM, HEAD_DIM), BF16)
    scale = HEAD_DIM ** -0.5

    def pair_body(pi, carry):
        cs = []
        for e in range(PAIR):
            ci = pi * PAIR + e
            cs.append((n_chunks - 1 - ci) if reverse else ci)
        rows = [pl.ds(pl.multiple_of(c * CHUNK, CHUNK), CHUNK) for c in cs]
        gbs = [gb_ref[r, :] for r in rows]
        gqs = [gq_ref[c] for c in cs]
        edge = 0 if reverse else CHUNK - 1
        units = [(e, qd) for e in range(PAIR) for qd in quads]
        cols = [slice(qd * QD, (qd + 1) * QD) for qd in quads]

        def per_head(group, e, qd):
            l0 = group * 2 * heads + direction * heads + qd * QUAD
            return jnp.concatenate(
                [jnp.broadcast_to(gbs[e][:, l0 + j:l0 + j + 1], (CHUNK, HEAD_DIM)) for j in range(QUAD)], axis=1)

        def per_head_narrow(group, e, qd):
            l0 = group * 2 * heads + direction * heads + qd * QUAD
            out = jnp.broadcast_to(gbs[e][:, l0 + QUAD - 1:l0 + QUAD], (CHUNK, QW))
            for j in range(QUAD - 2, -1, -1):
                col = jnp.broadcast_to(gbs[e][:, l0 + j:l0 + j + 1], (CHUNK, QW))
                out = jnp.where(jq < (j + 1) * CHUNK, col, out)
            return out

        k = [k_ref[rows[e], cols[qd]] for e, qd in units]
        beta = [per_head(0, e, qd) for e, qd in units]
        eg = [per_head(2, e, qd) for e, qd in units]
        kb = [k[u] * beta[u] for u in range(len(units))]
        k16 = [x.astype(BF16) for x in k]
        qs = [q_ref[rows[e], cols[qd]] * scale for e, qd in units]

        kq = [_dot_nt(jnp.concatenate([kb[u].astype(BF16), qs[u].astype(BF16)], axis=0),
                      _tile_rows(k16[u]) * k_mask) for u in range(len(units))]
        decay = []
        for e, qd in units:
            grow = gqs[e][direction * n_quads + qd:direction * n_quads + qd + 1, :]
            decay.append(jnp.where(causal, jnp.exp(jnp.minimum(per_head_narrow(1, e, qd) - grow, 0.0)), 0.0))
        qk16 = [(kq[u][CHUNK:] * decay[u]).astype(BF16) for u in range(len(units))]

        t = [eye for _ in units]
        p = [-jnp.where(strict, kq[u][:CHUNK] * decay[u], 0.0) for u in range(len(units))]
        n_steps = CHUNK.bit_length() - 1
        for step in range(n_steps):
            res = [_quad_inverse_step(t[u], p[u], bd_mask, step == 0, step == n_steps - 1) for u in range(len(units))]
            t = [r[0] for r in res]
            p = [r[1] for r in res]

        uw = []
        for u, (e, qd) in enumerate(units):
            vb16 = (v_ref[rows[e], cols[qd]] * beta[u]).astype(BF16)
            kbe16 = (kb[u] * eg[u]).astype(BF16)
            rhs = jnp.concatenate(
                [jnp.concatenate([vb16[:, j * HEAD_DIM:(j + 1) * HEAD_DIM],
                                  kbe16[:, j * HEAD_DIM:(j + 1) * HEAD_DIM]], axis=1) for j in range(QUAD)], axis=0)
            uw.append(_dot(_tile_rows(t[u].astype(BF16)) * bd_mask, rhs))

        for e in range(PAIR):
            base = e * n_quads
            s_old = [s_ref[h] for h in range(heads)]
            v_new = [None] * heads
            o_inter = [None] * heads
            for qd in quads:
                qd16 = (qs[base + qd] * eg[base + qd]).astype(BF16)
                uw_q = uw[base + qd]
                for pr in range(QUAD // 2):
                    j0, j1 = 2 * pr, 2 * pr + 1
                    h0, h1 = qd * QUAD + j0, qd * QUAD + j1
                    w16 = jnp.concatenate([uw_q[j0 * CHUNK:(j0 + 1) * CHUNK, HEAD_DIM:],
                                           uw_q[j1 * CHUNK:(j1 + 1) * CHUNK, HEAD_DIM:]], axis=1).astype(BF16)
                    lhs = jnp.concatenate([w16, qd16[:, j0 * HEAD_DIM:(j1 + 1) * HEAD_DIM]], axis=0)
                    bd_s = jnp.concatenate(
                        [jnp.concatenate([s_old[h0].astype(BF16), zeros_s], axis=1),
                         jnp.concatenate([zeros_s, s_old[h1].astype(BF16)], axis=1)], axis=0)
                    ws = _dot(lhs, bd_s)
                    for j, h in ((j0, h0), (j1, h1)):
                        lanes = slice((j - j0) * HEAD_DIM, (j - j0 + 1) * HEAD_DIM)
                        v_new[h] = uw_q[j * CHUNK:(j + 1) * CHUNK, :HEAD_DIM] - ws[:CHUNK, lanes]
                        o_inter[h] = ws[CHUNK:, lanes]
            v16 = [jnp.concatenate([v_new[qd * QUAD + j].astype(BF16) for j in range(QUAD)], axis=0) for qd in quads]
            o_intra = [_dot(_tile_rows(qk16[base + qd]) * bd_mask, v16[qd]) for qd in quads]

            s_upd = []
            for qd in quads:
                kd16 = (k[base + qd] * per_head(3, e, qd)).astype(BF16)
                kd_rows = jnp.concatenate([kd16[:, j * HEAD_DIM:(j + 1) * HEAD_DIM] for j in range(QUAD)], axis=0)
                s_upd.append(_dot_tn(kd_rows, jnp.concatenate([v16[qd]] * QUAD, axis=1) * k_mask))

            for qd in quads:
                for j in range(QUAD):
                    h = qd * QUAD + j
                    hc = slice(h * HEAD_DIM, (h + 1) * HEAD_DIM)
                    lg = 4 * heads + direction * heads + h
                    g_tot = gbs[e][edge:edge + 1, lg:lg + 1]
                    s_ref[h] = s_old[h] * g_tot + s_upd[qd][:, j * HEAD_DIM:(j + 1) * HEAD_DIM]
                    o = o_inter[h] + o_intra[qd][j * CHUNK:(j + 1) * CHUNK]
                    if finalize:
                        o = o + oprev_ref[rows[e], hc]
                        y = o * lax.rsqrt(jnp.mean(o * o, axis=-1, keepdims=True) + EPS) * na_ref[...]
                        o_ref[rows[e], hc] = (y * _silu(z_ref[rows[e], hc])).astype(o_ref.dtype)
                    else:
                        o_ref[rows[e], hc] = o
        return carry

    lax.fori_loop(0, n_chunks // PAIR, pair_body, 0)

    if emit_state:
        @pl.when(n == pl.num_programs(1) - 1)
        def _():
            sout_ref[...] = s_ref[...]


def _delta_scan(qkv, gb, gq, *, heads, seq_len, n_seq, row0, tb, reverse, s0=None, s0_index=None,
                o_prev=None, p=None, z_block=None, norm_a_l=None, out=None, emit_state=False):
    bw = heads * HEAD_DIM
    m = qkv.shape[0]
    ntb = seq_len // tb
    blk0 = row0 // tb
    finalize = o_prev is not None

    def tok(b, n):
        step = (ntb - 1 - n) if reverse else n
        return blk0 + b * ntb + step

    in_specs = [
        pl.BlockSpec((tb, bw), lambda b, n: (tok(b, n), 0)),
        pl.BlockSpec((tb, bw), lambda b, n: (tok(b, n), 1)),
        pl.BlockSpec((tb, bw), lambda b, n: (tok(b, n), 2)),
        pl.BlockSpec((tb, LANES), lambda b, n: (tok(b, n), 0)),
        pl.BlockSpec((tb // CHUNK, 2 * heads // QUAD, QW), lambda b, n: (tok(b, n), 0, 0)),
    ]
    args = [qkv, qkv, qkv, gb, gq]
    if s0 is not None:
        layer, direction = s0_index
        in_specs.append(pl.BlockSpec((None, None, None, heads, HEAD_DIM, HEAD_DIM),
                                     lambda b, n: (b, layer, direction, 0, 0, 0)))
        args.append(s0)
    if finalize:
        in_specs += [
            pl.BlockSpec((tb, bw), lambda b, n: (tok(b, n), 0)),
            pl.BlockSpec((tb, bw), lambda b, n: (tok(b, n), z_block)),
            pl.BlockSpec((1, HEAD_DIM), lambda b, n: (0, 0)),
        ]
        args += [o_prev, p, norm_a_l.reshape(1, HEAD_DIM)]
    in_specs, args, aliases = _with_alias(in_specs, args, out)
    out_specs = [pl.BlockSpec((tb, bw), lambda b, n: (tok(b, n), 0))]
    out_shape = [jax.ShapeDtypeStruct((m, bw), BF16 if finalize else F32)]
    if emit_state:
        out_specs.append(pl.BlockSpec((None, heads, HEAD_DIM, HEAD_DIM), lambda b, n: (b, 0, 0, 0)))
        out_shape.append(jax.ShapeDtypeStruct((n_seq, heads, HEAD_DIM, HEAD_DIM), F32))
    kern = functools.partial(_delta_kernel, heads=heads, reverse=reverse, has_s0=s0 is not None,
                             finalize=finalize, emit_state=emit_state, has_alias=out is not None)
    res = pl.pallas_call(
        kern,
        grid=(n_seq, ntb),
        in_specs=in_specs,
        out_specs=out_specs,
        out_shape=out_shape,
        scratch_shapes=[pltpu.VMEM((heads, HEAD_DIM, HEAD_DIM), F32)],
        input_output_aliases=aliases,
        compiler_params=_params("parallel", "arbitrary"),
        name="delta_scan_bwd" if reverse else "delta_scan_fwd",
    )(*args)
    return res if emit_state else (res[0], None)


def _rope(x, cosf, sins):
    return x * cosf + pltpu.roll(x, HEAD_DIM // 2, 1) * sins


def _rms_head(x, g):
    return x * lax.rsqrt(jnp.mean(x * x, axis=-1, keepdims=True) + EPS) * g


def _attn_kernel(*refs, band, has_ctx, use_sink, use_norm, use_rope, emit_k, has_alias, t_self, s_ctx, tq, n_sub,
                 grp, key_chunk):
    it = iter(refs)
    sink_ref = next(it) if use_sink else None
    q_ref, k_ref, v_ref, z_ref = (next(it) for _ in range(4))
    if has_ctx:
        kc_ref, vc_ref = next(it), next(it)
    if use_norm:
        qn_ref, kn_ref = next(it), next(it)
    if use_rope:
        cq_ref, sq_ref, ck_ref, sk_ref = (next(it) for _ in range(4))
    if has_alias:
        next(it)
    y_ref = next(it)
    kout_ref = next(it) if emit_k else None
    ks_ref, vs_ref = next(it), next(it)

    h = pl.program_id(1)
    i = pl.program_id(2)
    pad = WINDOW if band else 0
    ctx0 = t_self + 2 * pad

    @pl.when(i == 0)
    def _():
        k = k_ref[...]
        if use_norm:
            k = _rms_head(k, kn_ref[...])
        if emit_k:
            kout_ref[...] = k
        if use_rope:
            k = _rope(k, ck_ref[...], sk_ref[...])
        def ones_lane(n_rows):
            return jnp.where(lax.broadcasted_iota(jnp.int32, (n_rows, HEAD_DIM), 1) == 0, 1.0, 0.0).astype(BF16)

        ks_ref[pad:pad + t_self, :] = k.astype(BF16)
        vs_ref[pad:pad + t_self, :HEAD_DIM] = v_ref[...].astype(BF16)
        vs_ref[pad:pad + t_self, HEAD_DIM:] = ones_lane(t_self)
        if band:
            ks_ref[0:pad, :] = jnp.zeros((pad, HEAD_DIM), BF16)
            vs_ref[0:pad, :] = jnp.zeros((pad, 2 * HEAD_DIM), BF16)
            ks_ref[pad + t_self:ctx0, :] = jnp.zeros((pad, HEAD_DIM), BF16)
            vs_ref[pad + t_self:ctx0, :] = jnp.zeros((pad, 2 * HEAD_DIM), BF16)
        if has_ctx:
            ks_ref[ctx0:ctx0 + s_ctx, :] = kc_ref[...].astype(BF16)
            vs_ref[ctx0:ctx0 + s_ctx, :HEAD_DIM] = vc_ref[...].astype(BF16)
            vs_ref[ctx0:ctx0 + s_ctx, HEAD_DIM:] = ones_lane(s_ctx)

    scale = HEAD_DIM ** -0.5 * LOG2E
    rows = grp * tq
    if band:
        width = 3 * WINDOW
        krel = lax.broadcasted_iota(jnp.int32, (rows, width), 1) - WINDOW
        in_band = jnp.abs((lax.broadcasted_iota(jnp.int32, (rows, width), 0) & (tq - 1)) - krel) <= WINDOW

    def q_block(sub):
        rs = slice(sub * tq, (sub + 1) * tq)
        qi = i * n_sub + sub
        qs = []
        for g in range(grp):
            q = q_ref[rs, g * HEAD_DIM:(g + 1) * HEAD_DIM]
            if use_norm:
                q = _rms_head(q, qn_ref[...])
            if use_rope:
                q = _rope(q, cq_ref[rs, :], sq_ref[rs, :])
            qs.append((q * scale).astype(BF16))
        q16 = jnp.concatenate(qs, axis=0)

        if band:
            r0 = pl.multiple_of(qi * tq, tq)
            kpos = krel + qi * tq
            valid = in_band & (kpos >= 0) & (kpos < t_self)
            pieces = [(lambda: jnp.where(valid, _dot_nt(q16, ks_ref[pl.ds(r0, width), :]), NEG),
                       lambda: vs_ref[pl.ds(r0, width), :])]
            bounds = [(ctx0, ctx0 + s_ctx)] if has_ctx else []
        else:
            step = min(t_self, key_chunk)
            bounds = [(a, a + step) for a in range(0, t_self, step)]
            if has_ctx:
                bounds.append((t_self, t_self + s_ctx))
            pieces = []
        for a, b in bounds:
            pieces.append((lambda a=a, b=b: _dot_nt(q16, ks_ref[a:b, :]), lambda a=a, b=b: vs_ref[a:b, :]))

        if use_sink:
            rid = lax.broadcasted_iota(jnp.int32, (rows, 1), 0)
            sink = jnp.zeros((rows, 1), F32)
            for g in range(grp):
                sink = jnp.where(rid >= g * tq, sink_ref[h * grp + g] * LOG2E, sink)
            m = sink
            acc = jnp.where(lax.broadcasted_iota(jnp.int32, (rows, 2 * HEAD_DIM), 1) == HEAD_DIM, 1.0, 0.0)
        else:
            m = acc = None
        for scores, values in pieces:
            s = scores()
            m_new = jnp.max(s, axis=-1, keepdims=True)
            if m is not None:
                m_new = jnp.maximum(m, m_new)
            pv = _dot(jnp.exp2(s - m_new).astype(BF16), values())
            acc = pv if m is None else jnp.exp2(m - m_new) * acc + pv
            m = m_new
        o = acc[:, :HEAD_DIM] / acc[:, HEAD_DIM:HEAD_DIM + 1]
        for g in range(grp):
            cols = slice(g * HEAD_DIM, (g + 1) * HEAD_DIM)
            y_ref[rs, cols] = (o[g * tq:(g + 1) * tq] * _silu(z_ref[rs, cols])).astype(y_ref.dtype)

    for sub in range(n_sub):
        q_block(sub)


def _attention(p, *, q_block, k_block, v_block, z_block, seq_len, n_seq, row0, tq, heads, band, out,
               ctx_k=None, ctx_v=None, layer=None, sink_l=None, q_norm_l=None, k_norm_l=None, rope=None,
               emit_k=False):
    m = p.shape[0]
    bw = heads * HEAD_DIM
    grp = heads // KV_HEADS
    gw = grp * HEAD_DIM
    n_sub = 4
    while seq_len % (n_sub * tq):
        n_sub //= 2
    tqs = n_sub * tq
    nq = seq_len // tqs
    qblk0 = row0 // tqs
    sblk0 = row0 // seq_len
    has_ctx = ctx_k is not None
    s_ctx = ctx_k.shape[2] if has_ctx else 0
    use_sink = sink_l is not None
    use_norm = q_norm_l is not None
    use_rope = rope is not None
    pad = WINDOW if band else 0
    assert seq_len <= KEY_CHUNK or seq_len % KEY_CHUNK == 0

    in_specs, args = [], []
    if use_sink:
        in_specs.append(pl.BlockSpec(memory_space=pltpu.SMEM))
        args.append(sink_l)
    in_specs += [
        pl.BlockSpec((tqs, gw), lambda b, h, i: (qblk0 + b * nq + i, q_block + h)),
        pl.BlockSpec((seq_len, HEAD_DIM), lambda b, h, i: (sblk0 + b, k_block + h)),
        pl.BlockSpec((seq_len, HEAD_DIM), lambda b, h, i: (sblk0 + b, v_block + h)),
        pl.BlockSpec((tqs, gw), lambda b, h, i: (qblk0 + b * nq + i, z_block + h)),
    ]
    args += [p, p, p, p]
    if has_ctx:
        spec = pl.BlockSpec((None, None, s_ctx, HEAD_DIM), lambda b, h, i: (b, layer, 0, h))
        in_specs += [spec, spec]
        args += [ctx_k, ctx_v]
    if use_norm:
        spec = pl.BlockSpec((1, HEAD_DIM), lambda b, h, i: (0, 0))
        in_specs += [spec, spec]
        args += [q_norm_l.reshape(1, HEAD_DIM), k_norm_l.reshape(1, HEAD_DIM)]
    if use_rope:
        cosf, sins = rope
        qspec = pl.BlockSpec((tqs, HEAD_DIM), lambda b, h, i: (i, 0))
        kspec = pl.BlockSpec((seq_len, HEAD_DIM), lambda b, h, i: (0, 0))
        in_specs += [qspec, qspec, kspec, kspec]
        args += [cosf, sins, cosf, sins]
    in_specs, args, aliases = _with_alias(in_specs, args, out)
    out_specs = [pl.BlockSpec((tqs, gw), lambda b, h, i: (qblk0 + b * nq + i, h))]
    out_shape = [jax.ShapeDtypeStruct((m, bw), BF16)]
    if emit_k:
        out_specs.append(pl.BlockSpec((None, seq_len, HEAD_DIM), lambda b, h, i: (b, 0, h)))
        out_shape.append(jax.ShapeDtypeStruct((n_seq, seq_len, KV_W), F32))
    kern = functools.partial(_attn_kernel, band=band, has_ctx=has_ctx, use_sink=use_sink, use_norm=use_norm,
                             use_rope=use_rope, emit_k=emit_k, has_alias=out is not None, t_self=seq_len,
                             s_ctx=s_ctx, tq=tq, n_sub=n_sub, grp=grp, key_chunk=KEY_CHUNK)
    n_keys = seq_len + 2 * pad + s_ctx
    assert tq & (tq - 1) == 0 and (not band or tq == WINDOW)
    res = pl.pallas_call(
        kern,
        grid=(n_seq, KV_HEADS, nq),
        in_specs=in_specs,
        out_specs=out_specs,
        out_shape=out_shape,
        scratch_shapes=[pltpu.VMEM((n_keys, HEAD_DIM), BF16), pltpu.VMEM((n_keys, 2 * HEAD_DIM), BF16)],
        input_output_aliases=aliases,
        compiler_params=_params("parallel", "parallel", "arbitrary"),
        name=("attn_band" if band else "attn_full") + ("_ctx" if has_ctx else ""),
    )(*args)
    return res if emit_k else (res[0], None)


def _merge_kernel(ya_ref, yb_ref, yc_ref, w_ref, ga_ref, gb_ref, gc_ref, o_ref):
    acc = jax.nn.sigmoid(ga_ref[...]) * _dot(ya_ref[...], w_ref[0])
    acc = acc + jax.nn.sigmoid(gb_ref[...]) * _dot(yb_ref[...], w_ref[1])
    acc = acc + jax.nn.sigmoid(gc_ref[...]) * _dot(yc_ref[...], w_ref[2])
    o_ref[...] = acc.astype(o_ref.dtype)


def _merge(ya, yb, yc, w_branch_l, p, gate_col0, tm, tn):
    m, bw = ya.shape
    d = w_branch_l.shape[2]
    g0 = gate_col0 // tn
    gstep = d // tn
    br = pl.BlockSpec((tm, bw), lambda i, j: (i, 0))
    return pl.pallas_call(
        _merge_kernel,
        grid=(m // tm, d // tn),
        in_specs=[
            br, br, br,
            pl.BlockSpec((N_BRANCH, bw, tn), lambda i, j: (0, 0, j)),
            pl.BlockSpec((tm, tn), lambda i, j: (i, g0 + j)),
            pl.BlockSpec((tm, tn), lambda i, j: (i, g0 + gstep + j)),
            pl.BlockSpec((tm, tn), lambda i, j: (i, g0 + 2 * gstep + j)),
        ],
        out_specs=pl.BlockSpec((tm, tn), lambda i, j: (i, j)),
        out_shape=jax.ShapeDtypeStruct((m, d), BF16),
        compiler_params=_params("parallel", "arbitrary"),
        name="merge",
    )(ya, yb, yc, w_branch_l, p, p, p)


def _out_kernel(a_ref, w_ref, x_ref, mod_ref, o_ref):
    o_ref[...] = x_ref[...] + mod_ref[2:3, :] * _dot(a_ref[...], w_ref[...])


def _out_proj(mixed, w_out_l, x, mod_l, m_ctx, t_lat, ctx_row, tm, tn):
    m, d = x.shape
    return pl.pallas_call(
        _out_kernel,
        grid=(m // tm, d // tn),
        in_specs=[
            pl.BlockSpec((tm, d), lambda i, j: (i, 0)),
            pl.BlockSpec((d, tn), lambda i, j: (0, j)),
            pl.BlockSpec((tm, tn), lambda i, j: (i, j)),
            pl.BlockSpec((None, 3, tn), lambda i, j: (_mod_row(i, tm, m_ctx, t_lat, ctx_row), 0, j)),
        ],
        out_specs=pl.BlockSpec((tm, tn), lambda i, j: (i, j)),
        out_shape=jax.ShapeDtypeStruct((m, d), F32),
        compiler_params=_params("parallel", "arbitrary"),
        name="out_proj",
    )(mixed, w_out_l, x, mod_l)


def _final_kernel(x_ref, g_ref, o_ref):
    x = x_ref[...]
    o_ref[...] = x * lax.rsqrt(jnp.mean(x * x, axis=-1, keepdims=True) + EPS) * g_ref[...]


def _final_norm(x, g, row0, n_rows, tile):
    d = x.shape[1]
    blk0 = row0 // tile
    return pl.pallas_call(
        _final_kernel,
        grid=(n_rows // tile,),
        in_specs=[pl.BlockSpec((tile, d), lambda i: (blk0 + i, 0)), pl.BlockSpec((1, d), lambda i: (0, 0))],
        out_specs=pl.BlockSpec((tile, d), lambda i: (i, 0)),
        out_shape=jax.ShapeDtypeStruct((n_rows, d), F32),
        compiler_params=_params("parallel"),
        name="final_norm",
    )(x, g.reshape(1, d))


def _axial_rope_tables(n_tokens):
    rows = n_tokens // GRID_W
    row = jnp.repeat(jnp.arange(rows), GRID_W).astype(F32)
    col = jnp.tile(jnp.arange(GRID_W), rows).astype(F32)
    n_freq = HEAD_DIM // 4
    inv = ROPE_THETA ** (-jnp.arange(n_freq, dtype=F32) / n_freq)
    ang = jnp.concatenate([row[:, None] * inv, col[:, None] * inv], axis=-1)
    cos, sin = jnp.cos(ang), jnp.sin(ang)
    return jnp.concatenate([cos, cos], axis=-1), jnp.concatenate([-sin, sin], axis=-1)


def kernel(x_prompt, x_sample, state_delta, cache_win_k, cache_win_v, cache_glob_k, cache_glob_v, c, c_ctx,
           norm_g, w_mod, b_mod, w_in, conv_w, a_log, dt_bias, norm_a, sink, q_norm, k_norm, w_branch, w_out,
           final_g):
    n_layers, d = norm_g.shape
    bw = d // 2
    heads = bw // HEAD_DIM
    nbc, s_len, _ = x_prompt.shape
    nbl, t_len, _ = x_sample.shape
    past = cache_win_k.shape[2]
    m_ctx, m_lat = nbc * s_len, nbl * t_len
    m = m_ctx + m_lat
    assert m_ctx % t_len == 0 and t_len % s_len == 0 and s_len % (2 * WINDOW) == 0 and 8 * heads <= LANES
    assert heads % KV_HEADS == 0 and heads % QUAD == 0 and bw % KV_W == 0

    widths = dict(a_qkv=3 * bw, a_z=bw, b_q=bw, b_k=KV_W, b_v=KV_W, b_z=bw, c_q=bw, c_k=KV_W, c_v=KV_W, c_z=bw,
                  gates=N_BRANCH * d)
    off, col = {}, 0
    for name, wd in widths.items():
        off[name] = col
        col += wd
    n_main = col
    n_ba = 4 * heads
    src = 4 * bw

    tile_seq = s_len
    tm_big = _pick_tile(math.gcd(m_ctx, t_len), (1024, 512, 256))
    tm_merge = _pick_tile(m, (1024, 512))
    tn = _pick_tile(math.gcd(d, KV_W), (512,))
    tn_wide = _pick_tile(math.gcd(d, n_main), (1024, tn))
    tb = 4 * CHUNK

    w_in16 = w_in.astype(BF16)
    w_main = jnp.concatenate([w_in16[:, :, :src], w_in16[:, :, src + n_ba:]], axis=-1)
    w_ba = jnp.pad(w_in16[:, :, src:src + n_ba], ((0, 0), (0, 0), (0, LANES - n_ba)))
    w_branch16 = w_branch.astype(BF16)
    w_out16 = w_out.astype(BF16)

    ctx_row = nbl
    mod_rows = -(-(nbl + 1) // 8) * 8
    c_all = jnp.concatenate([c, c_ctx[None], jnp.zeros((mod_rows - nbl - 1, d), F32)], axis=0)
    mod = _modulation(c_all, w_mod, b_mod).reshape(n_layers, mod_rows, 3, d)

    rope = _axial_rope_tables(t_len)
    ckw = cache_win_k.reshape(nbl, n_layers, past, KV_W)
    cvw = cache_win_v.reshape(nbl, n_layers, past, KV_W)
    ckg = cache_glob_k.reshape(nbl, n_layers, past, KV_W)
    cvg = cache_glob_v.reshape(nbl, n_layers, past, KV_W)

    x = jnp.concatenate([x_prompt.reshape(m_ctx, d), x_sample.reshape(m_lat, d)], axis=0)
    states, win_k, win_v, glob_k, glob_v = [], [], [], [], []
    for l in range(n_layers):
        h = _prep(x, mod[l], norm_g[l], m_ctx, t_len, ctx_row, tile_seq)
        p = _matmul(h, w_main[l], tm_big, tn_wide, F32, "in_proj")
        ba = _matmul(h, w_ba[l], tm_big, LANES, F32, "in_proj_gates")

        qkv = _delta_conv(p, conv_w[l], s_len, nbc, 0, None, bw)
        qkv = _delta_conv(p, conv_w[l], t_len, nbl, m_ctx // t_len, qkv, bw)
        gb = _delta_gates(ba, a_log[l], dt_bias[l], heads, tile_seq)
        gq = jnp.swapaxes(gb.reshape(m // CHUNK, CHUNK, LANES)[:, :, 2 * heads:4 * heads], 1, 2)
        gq = gq.reshape(m // CHUNK, 2 * heads // QUAD, QW)
        common = dict(heads=heads, tb=tb)
        fin = dict(p=p, z_block=off["a_z"] // bw, norm_a_l=norm_a[l])
        o_f, s_f = _delta_scan(qkv, gb, gq, seq_len=s_len, n_seq=nbc, row0=0, reverse=False,
                               emit_state=True, **common)
        o_f, _ = _delta_scan(qkv, gb, gq, seq_len=t_len, n_seq=nbl, row0=m_ctx, reverse=False, out=o_f,
                             s0=state_delta, s0_index=(l, 0), **common)
        ya, s_b = _delta_scan(qkv, gb, gq, seq_len=s_len, n_seq=nbc, row0=0, reverse=True,
                              o_prev=o_f, emit_state=True, **common, **fin)
        ya, _ = _delta_scan(qkv, gb, gq, seq_len=t_len, n_seq=nbl, row0=m_ctx, reverse=True, out=ya,
                            s0=state_delta, s0_index=(l, 1), o_prev=o_f, **common, **fin)
        states.append(jnp.stack([s_f, s_b], axis=1))

        gw = (heads // KV_HEADS) * HEAD_DIM
        blocks_b = dict(q_block=off["b_q"] // gw, k_block=off["b_k"] // HEAD_DIM, v_block=off["b_v"] // HEAD_DIM,
                        z_block=off["b_z"] // gw)
        blocks_c = dict(q_block=off["c_q"] // gw, k_block=off["c_k"] // HEAD_DIM, v_block=off["c_v"] // HEAD_DIM,
                        z_block=off["c_z"] // gw)
        ctx_args = dict(seq_len=s_len, n_seq=nbc, row0=0, tq=s_len, heads=heads, band=False, out=None)
        lat_args = dict(seq_len=t_len, n_seq=nbl, row0=m_ctx, tq=WINDOW, heads=heads, layer=l, rope=rope)
        yb, _ = _attention(p, sink_l=sink[l], **blocks_b, **ctx_args)
        yb, _ = _attention(p, out=yb, sink_l=sink[l], band=True, ctx_k=ckw, ctx_v=cvw, **blocks_b, **lat_args)
        yc, kn = _attention(p, q_norm_l=q_norm[l], k_norm_l=k_norm[l], emit_k=True, **blocks_c, **ctx_args)
        yc, _ = _attention(p, out=yc, q_norm_l=q_norm[l], k_norm_l=k_norm[l], band=False, ctx_k=ckg, ctx_v=cvg,
                           **blocks_c, **lat_args)

        def ctx_slab(name):
            return p[:m_ctx, off[name]:off[name] + KV_W].reshape(nbc, s_len, KV_HEADS, HEAD_DIM)

        win_k.append(ctx_slab("b_k"))
        win_v.append(ctx_slab("b_v"))
        glob_k.append(kn.reshape(nbc, s_len, KV_HEADS, HEAD_DIM))
        glob_v.append(ctx_slab("c_v"))

        mixed = _merge(ya, yb, yc, w_branch16[l], p, off["gates"], tm_merge, tn)
        x = _out_proj(mixed, w_out16[l], x, mod[l], m_ctx, t_len, ctx_row, tm_big, tn_wide)

    y_prompt = _final_norm(x, final_g, 0, m_ctx, tile_seq).reshape(nbc, s_len, d)
    y_sample = _final_norm(x, final_g, m_ctx, m_lat, tile_seq).reshape(nbl, t_len, d)
    return (y_prompt, y_sample, jnp.stack(states, axis=1), jnp.stack(win_k, axis=1), jnp.stack(win_v, axis=1),
            jnp.stack(glob_k, axis=1), jnp.stack(glob_v, axis=1))
```

```python
import functools
import math

import jax
import jax.numpy as jnp
import numpy as np
from jax import lax
from jax.experimental import pallas as pl
from jax.experimental.pallas import tpu as pltpu

HEAD_DIM = 128
KV_HEADS = 4
KV_W = KV_HEADS * HEAD_DIM
CHUNK = 64
WINDOW = 128
GRID_W = 64
CONV_K = 3
N_BRANCH = 3
ROPE_THETA = 10000.0
EPS = 1e-6
NEG = -1e30
LOG2E = math.log2(math.e)
KEY_CHUNK = 512

LANES = 128
VMEM_LIMIT_BYTES = 56 * 1024 * 1024

F32 = jnp.float32
BF16 = jnp.bfloat16


def _params(*sem):
    return pltpu.CompilerParams(dimension_semantics=sem, vmem_limit_bytes=VMEM_LIMIT_BYTES)


def _dot(a, b):
    return jnp.dot(a, b, preferred_element_type=F32)


def _dot_nt(a, b):
    return lax.dot_general(a, b, (((1,), (1,)), ((), ())), preferred_element_type=F32)


def _dot_tn(a, b):
    return lax.dot_general(a, b, (((0,), (0,)), ((), ())), preferred_element_type=F32)


def _silu(x):
    return x * jax.nn.sigmoid(x)


def _with_alias(in_specs, args, out):
    if out is None:
        return in_specs, args, {}
    return in_specs + [pl.BlockSpec(memory_space=pl.ANY)], args + [out], {len(args): 0}


def _pick_tile(n, candidates):
    for c in candidates:
        if n % c == 0:
            return c
    raise ValueError(f"no tile in {candidates} divides {n}")


MOD_K_TILE = 256


def _mod_kernel(c_ref, w_ref, b_ref, o_ref):
    @pl.when(pl.program_id(1) == 0)
    def _():
        o_ref[...] = jnp.broadcast_to(b_ref[...], o_ref.shape)

    a = _silu(c_ref[...]).astype(BF16)
    o_ref[...] += _dot(a, w_ref[...].astype(BF16))


def _modulation(c_all, w_mod, b_mod):
    n_layers, d, n3 = w_mod.shape
    rows = c_all.shape[0]
    kt = _pick_tile(d, (MOD_K_TILE, LANES))
    return pl.pallas_call(
        _mod_kernel,
        grid=(n_layers, d // kt),
        in_specs=[
            pl.BlockSpec((rows, kt), lambda l, k: (0, k)),
            pl.BlockSpec((None, kt, n3), lambda l, k: (l, k, 0)),
            pl.BlockSpec((None, 1, n3), lambda l, k: (l, 0, 0)),
        ],
        out_specs=pl.BlockSpec((None, rows, n3), lambda l, k: (l, 0, 0)),
        out_shape=jax.ShapeDtypeStruct((n_layers, rows, n3), F32),
        compiler_params=_params("parallel", "arbitrary"),
        name="modulation",
    )(c_all, w_mod, b_mod.reshape(n_layers, 1, n3))


def _prep_kernel(x_ref, mod_ref, g_ref, o_ref):
    x = x_ref[...]
    y = x * lax.rsqrt(jnp.mean(x * x, axis=-1, keepdims=True) + EPS) * g_ref[...]
    shift = mod_ref[0:1, :]
    scale = mod_ref[1:2, :]
    o_ref[...] = (y * (1.0 + scale) + shift).astype(o_ref.dtype)


def _mod_row(i, tile, m_ctx, t_lat, ctx_row):
    start = i * tile
    return jnp.where(start < m_ctx, ctx_row, (start - m_ctx) // t_lat)


def _prep(x, mod_l, norm_g_l, m_ctx, t_lat, ctx_row, tile):
    m, d = x.shape
    return pl.pallas_call(
        _prep_kernel,
        grid=(m // tile,),
        in_specs=[
            pl.BlockSpec((tile, d), lambda i: (i, 0)),
            pl.BlockSpec((None, 3, d), lambda i: (_mod_row(i, tile, m_ctx, t_lat, ctx_row), 0, 0)),
            pl.BlockSpec((1, d), lambda i: (0, 0)),
        ],
        out_specs=pl.BlockSpec((tile, d), lambda i: (i, 0)),
        out_shape=jax.ShapeDtypeStruct((m, d), BF16),
        compiler_params=_params("parallel"),
        name="prep",
    )(x, mod_l, norm_g_l.reshape(1, d))


def _mm_kernel(a_ref, w_ref, o_ref):
    o_ref[...] = _dot(a_ref[...], w_ref[...]).astype(o_ref.dtype)


def _matmul(a, w, tm, tn, out_dtype, name):
    m, k = a.shape
    _, n = w.shape
    return pl.pallas_call(
        _mm_kernel,
        grid=(m // tm, n // tn),
        in_specs=[
            pl.BlockSpec((tm, k), lambda i, j: (i, 0)),
            pl.BlockSpec((k, tn), lambda i, j: (0, j)),
        ],
        out_specs=pl.BlockSpec((tm, tn), lambda i, j: (i, j)),
        out_shape=jax.ShapeDtypeStruct((m, n), out_dtype),
        compiler_params=_params("parallel", "arbitrary"),
        name=name,
    )(a, w)


def _conv_kernel(x_ref, w_ref, *rest, n_qk_blocks):
    o_ref = rest[-1]
    x = x_ref[...]
    t = x.shape[0]
    row = lax.broadcasted_iota(jnp.int32, x.shape, 0)
    prev = jnp.where(row == 0, 0.0, pltpu.roll(x, 1, 0))
    nxt = jnp.where(row == t - 1, 0.0, pltpu.roll(x, t - 1, 0))
    y = _silu(prev * w_ref[0:1, :] + x * w_ref[1:2, :] + nxt * w_ref[2:3, :])
    is_qk = pl.program_id(1) < n_qk_blocks
    for s in range(x.shape[1] // HEAD_DIM):
        ys = y[:, s * HEAD_DIM:(s + 1) * HEAD_DIM]
        inv = lax.rsqrt(jnp.sum(ys * ys, axis=-1, keepdims=True) + EPS)
        o_ref[:, s * HEAD_DIM:(s + 1) * HEAD_DIM] = ys * jnp.where(is_qk, inv, 1.0)


CONV_BLOCK_BYTES = 4 * 1024 * 1024


def _delta_conv(p, conv_w_l, seq_len, n_seq, row_block0, out, bw):
    cols = HEAD_DIM
    while bw % (2 * cols) == 0 and seq_len * 2 * cols * 4 <= CONV_BLOCK_BYTES:
        cols *= 2
    n_col = 3 * bw // cols
    kern = functools.partial(_conv_kernel, n_qk_blocks=2 * bw // cols)
    m = p.shape[0]
    in_specs = [
        pl.BlockSpec((seq_len, cols), lambda b, j: (row_block0 + b, j)),
        pl.BlockSpec((CONV_K, cols), lambda b, j: (0, j)),
    ]
    args = [p, conv_w_l]
    in_specs, args, aliases = _with_alias(in_specs, args, out)
    return pl.pallas_call(
        kern,
        grid=(n_seq, n_col),
        in_specs=in_specs,
        out_specs=pl.BlockSpec((seq_len, cols), lambda b, j: (row_block0 + b, j)),
        out_shape=jax.ShapeDtypeStruct((m, 3 * bw), F32),
        input_output_aliases=aliases,
        compiler_params=_params("parallel", "parallel"),
        name="delta_conv",
    )(*args)


def _gates_kernel(ba_ref, alog_ref, dtb_ref, o_ref, *, heads):
    ba = ba_ref[...]
    rows = ba.shape[0]
    lane = lax.broadcasted_iota(jnp.int32, ba.shape, 1)
    pos = lax.broadcasted_iota(jnp.int32, ba.shape, 0) & (CHUNK - 1)
    beta = jax.nn.sigmoid(ba)
    z = ba + dtb_ref[...]
    softplus = jnp.maximum(z, 0.0) + jnp.log1p(jnp.exp(-jnp.abs(z)))
    g = -jnp.exp(alog_ref[...]) * softplus
    pre = g
    suf = g
    step = 1
    while step < CHUNK:
        pre = pre + jnp.where(pos >= step, pltpu.roll(pre, step, 0), 0.0)
        suf = suf + jnp.where(pos < CHUNK - step, pltpu.roll(suf, rows - step, 0), 0.0)
        step *= 2
    gc = jnp.where(lane < 3 * heads, pre, suf)
    total = pre + suf - g
    eg = jnp.exp(pltpu.roll(gc, 2 * heads, 1))
    ekd = jnp.exp(pltpu.roll(total - gc, 4 * heads, 1))
    o_ref[...] = jnp.where(lane < 2 * heads, beta,
                           jnp.where(lane < 4 * heads, gc, jnp.where(lane < 6 * heads, eg, ekd)))


def _delta_gates(ba, a_log_l, dt_bias_l, heads, tile):
    m = ba.shape[0]
    pad = jnp.zeros((LANES - 4 * heads,), F32)
    lead = jnp.zeros((2 * heads,), F32)
    alog = jnp.concatenate([lead, a_log_l.reshape(-1), pad]).reshape(1, LANES)
    dtb = jnp.concatenate([lead, dt_bias_l.reshape(-1), pad]).reshape(1, LANES)
    return pl.pallas_call(
        functools.partial(_gates_kernel, heads=heads),
        grid=(m // tile,),
        in_specs=[
            pl.BlockSpec((tile, LANES), lambda i: (i, 0)),
            pl.BlockSpec((1, LANES), lambda i: (0, 0)),
            pl.BlockSpec((1, LANES), lambda i: (0, 0)),
        ],
        out_specs=pl.BlockSpec((tile, LANES), lambda i: (i, 0)),
        out_shape=jax.ShapeDtypeStruct((m, LANES), F32),
        compiler_params=_params("parallel"),
        name="delta_gates",
    )(ba, alog, dtb)


QUAD = 4
PAIR = 2
QW = QUAD * CHUNK
QD = QUAD * HEAD_DIM


def _split_bf16(x):
    hi = x.astype(BF16)
    lo = (x - hi.astype(F32)).astype(BF16)
    return hi, lo


def _tile_rows(x):
    return jnp.concatenate([x] * QUAD, axis=0)


def _quad_inverse_step(t, p, bd_mask, first, last):
    p_hi, p_lo = _split_bf16(p)
    if first:
        x_hi, x_lo = p_hi, p_lo
    else:
        t_hi, t_lo = _split_bf16(t)
        x_hi = t_hi if last else jnp.concatenate([t_hi, p_hi], axis=0)
        x_lo = t_lo if last else jnp.concatenate([t_lo, p_lo], axis=0)
    bd_hi = _tile_rows(p_hi) * bd_mask
    bd_lo = _tile_rows(p_lo) * bd_mask
    lhs = jnp.concatenate([x_hi, x_lo, x_hi], axis=1)
    rhs = jnp.concatenate([bd_hi, bd_hi, bd_lo], axis=0)
    y = _dot(lhs, rhs)
    if first:
        return t + p, y
    return t + y[:CHUNK], (None if last else y[CHUNK:])


def _delta_kernel(*refs, heads, reverse, has_s0, finalize, emit_state, has_alias):
    it = iter(refs)
    q_ref, k_ref, v_ref, gb_ref, gq_ref = (next(it) for _ in range(5))
    s0_ref = next(it) if has_s0 else None
    if finalize:
        oprev_ref, z_ref, na_ref = (next(it) for _ in range(3))
    if has_alias:
        next(it)
    o_ref = next(it)
    sout_ref = next(it) if emit_state else None
    s_ref = next(it)

    n = pl.program_id(1)
    n_chunks = q_ref.shape[0] // CHUNK
    n_quads = heads // QUAD
    direction = 1 if reverse else 0
    quads = range(n_quads)

    @pl.when(n == 0)
    def _():
        if has_s0:
            s_ref[...] = s0_ref[...]
        else:
            s_ref[...] = jnp.zeros_like(s_ref)

    ii = lax.broadcasted_iota(jnp.int32, (CHUNK, QW), 0)
    jq = lax.broadcasted_iota(jnp.int32, (CHUNK, QW), 1)
    jj = jq & (CHUNK - 1)
    if reverse:
        causal, strict = ii <= jj, ii < jj
    else:
        causal, strict = ii >= jj, ii > jj
    eye = jnp.where(ii == jj, 1.0, 0.0)
    br = lax.broadcasted_iota(jnp.int32, (QW, QW), 0) // CHUNK
    bd_mask = jnp.where(br == lax.broadcasted_iota(jnp.int32, (QW, QW), 1) // CHUNK, 1.0, 0.0).astype(BF16)
    kr = lax.broadcasted_iota(jnp.int32, (QW, QD), 0) // CHUNK
    k_mask = jnp.where(kr == lax.broadcasted_iota(jnp.int32, (QW, QD), 1) // HEAD_DIM, 1.0, 0.0).astype(BF16)
    zeros_s = jnp.zeros((HEAD_DIM, HEAD_DIM), BF16)
    scale = HEAD_DIM ** -0.5

    def pair_body(pi, carry):
        cs = []
        for e in range(PAIR):
            ci = pi * PAIR + e
            cs.append((n_chunks - 1 - ci) if reverse else ci)
        rows = [pl.ds(pl.multiple_of(c * CHUNK, CHUNK), CHUNK) for c in cs]
        gbs = [gb_ref[r, :] for r in rows]
        gqs = [gq_ref[c] for c in cs]
        edge = 0 if reverse else CHUNK - 1
        units = [(e, qd) for e in range(PAIR) for qd in quads]
        cols = [slice(qd * QD, (qd + 1) * QD) for qd in quads]

        def per_head(group, e, qd):
            l0 = group * 2 * heads + direction * heads + qd * QUAD
            return jnp.concatenate(
                [jnp.broadcast_to(gbs[e][:, l0 + j:l0 + j + 1], (CHUNK, HEAD_DIM)) for j in range(QUAD)], axis=1)

        def per_head_narrow(group, e, qd):
            l0 = group * 2 * heads + direction * heads + qd * QUAD
            out = jnp.broadcast_to(gbs[e][:, l0 + QUAD - 1:l0 + QUAD], (CHUNK, QW))
            for j in range(QUAD - 2, -1, -1):
                col = jnp.broadcast_to(gbs[e][:, l0 + j:l0 + j + 1], (CHUNK, QW))
                out = jnp.where(jq < (j + 1) * CHUNK, col, out)
            return out

        k = [k_ref[rows[e], cols[qd]] for e, qd in units]
        beta = [per_head(0, e, qd) for e, qd in units]
        eg = [per_head(2, e, qd) for e, qd in units]
        kb = [k[u] * beta[u] for u in range(len(units))]
        k16 = [x.astype(BF16) for x in k]
        qs = [q_ref[rows[e], cols[qd]] * scale for e, qd in units]

        kq = [_dot_nt(jnp.concatenate([kb[u].astype(BF16), qs[u].astype(BF16)], axis=0),
                      _tile_rows(k16[u]) * k_mask) for u in range(len(units))]
        decay = []
        for e, qd in units:
            grow = gqs[e][direction * n_quads + qd:direction * n_quads + qd + 1, :]
            decay.append(jnp.where(causal, jnp.exp(jnp.minimum(per_head_narrow(1, e, qd) - grow, 0.0)), 0.0))
        qk16 = [(kq[u][CHUNK:] * decay[u]).astype(BF16) for u in range(len(units))]

        t = [eye for _ in units]
        p = [-jnp.where(strict, kq[u][:CHUNK] * decay[u], 0.0) for u in range(len(units))]
        n_steps = CHUNK.bit_length() - 1
        for step in range(n_steps):
            res = [_quad_inverse_step(t[u], p[u], bd_mask, step == 0, step == n_steps - 1) for u in range(len(units))]
            t = [r[0] for r in res]
            p = [r[1] for r in res]

        uw = []
        for u, (e, qd) in enumerate(units):
            vb16 = (v_ref[rows[e], cols[qd]] * beta[u]).astype(BF16)
            kbe16 = (kb[u] * eg[u]).astype(BF16)
            rhs = jnp.concatenate(
                [jnp.concatenate([vb16[:, j * HEAD_DIM:(j + 1) * HEAD_DIM],
                                  kbe16[:, j * HEAD_DIM:(j + 1) * HEAD_DIM]], axis=1) for j in range(QUAD)], axis=0)
            uw.append(_dot(_tile_rows(t[u].astype(BF16)) * bd_mask, rhs))

        for e in range(PAIR):
            base = e * n_quads
            s_old = [s_ref[h] for h in range(heads)]
            v_new = [None] * heads
            o_inter = [None] * heads
            for qd in quads:
                qd16 = (qs[base + qd] * eg[base + qd]).astype(BF16)
                uw_q = uw[base + qd]
                for pr in range(QUAD // 2):
                    j0, j1 = 2 * pr, 2 * pr + 1
                    h0, h1 = qd * QUAD + j0, qd * QUAD + j1
                    w16 = jnp.concatenate([uw_q[j0 * CHUNK:(j0 + 1) * CHUNK, HEAD_DIM:],
                                           uw_q[j1 * CHUNK:(j1 + 1) * CHUNK, HEAD_DIM:]], axis=1).astype(BF16)
                    lhs = jnp.concatenate([w16, qd16[:, j0 * HEAD_DIM:(j1 + 1) * HEAD_DIM]], axis=0)
                    bd_s = jnp.concatenate(
                        [jnp.concatenate([s_old[h0].astype(BF16), zeros_s], axis=1),
                         jnp.concatenate([zeros_s, s_old[h1].astype(BF16)], axis=1)], axis=0)
                    ws = _dot(lhs, bd_s)
                    for j, h in ((j0, h0), (j1, h1)):
                        lanes = slice((j - j0) * HEAD_DIM, (j - j0 + 1) * HEAD_DIM)
                        v_new[h] = uw_q[j * CHUNK:(j + 1) * CHUNK, :HEAD_DIM] - ws[:CHUNK, lanes]
                        o_inter[h] = ws[CHUNK:, lanes]
            v16 = [jnp.concatenate([v_new[qd * QUAD + j].astype(BF16) for j in range(QUAD)], axis=0) for qd in quads]
            o_intra = [_dot(_tile_rows(qk16[base + qd]) * bd_mask, v16[qd]) for qd in quads]

            s_upd = []
            for qd in quads:
                kd16 = (k[base + qd] * per_head(3, e, qd)).astype(BF16)
                kd_rows = jnp.concatenate([kd16[:, j * HEAD_DIM:(j + 1) * HEAD_DIM] for j in range(QUAD)], axis=0)
                s_upd.append(_dot_tn(kd_rows, jnp.concatenate([v16[qd]] * QUAD, axis=1) * k_mask))

            for qd in quads:
                for j in range(QUAD):
                    h = qd * QUAD + j
                    hc = slice(h * HEAD_DIM, (h + 1) * HEAD_DIM)
                    lg = 4 * heads + direction * heads + h
                    g_tot = gbs[e][edge:edge + 1, lg:lg + 1]
                    s_ref[h] = s_old[h] * g_tot + s_upd[qd][:, j * HEAD_DIM:(j + 1) * HEAD_DIM]
                    o = o_inter[h] + o_intra[qd][j * CHUNK:(j + 1) * CHUNK]
                    if finalize:
                        o = o + oprev_ref[rows[e], hc]
                        y = o * lax.rsqrt(jnp.mean(o * o, axis=-1, keepdims=True) + EPS) * na_ref[...]
                        o_ref[rows[e], hc] = (y * _silu(z_ref[rows[e], hc])).astype(o_ref.dtype)
                    else:
                        o_ref[rows[e], hc] = o
        return carry

    lax.fori_loop(0, n_chunks // PAIR, pair_body, 0)

    if emit_state:
        @pl.when(n == pl.num_programs(1) - 1)
        def _():
            sout_ref[...] = s_ref[...]


def _delta_scan(qkv, gb, gq, *, heads, seq_len, n_seq, row0, tb, reverse, s0=None, s0_index=None,
                o_prev=None, p=None, z_block=None, norm_a_l=None, out=None, emit_state=False):
    bw = heads * HEAD_DIM
    m = qkv.shape[0]
    ntb = seq_len // tb
    blk0 = row0 // tb
    finalize = o_prev is not None

    def tok(b, n):
        step = (ntb - 1 - n) if reverse else n
        return blk0 + b * ntb + step

    in_specs = [
        pl.BlockSpec((tb, bw), lambda b, n: (tok(b, n), 0)),
        pl.BlockSpec((tb, bw), lambda b, n: (tok(b, n), 1)),
        pl.BlockSpec((tb, bw), lambda b, n: (tok(b, n), 2)),
        pl.BlockSpec((tb, LANES), lambda b, n: (tok(b, n), 0)),
        pl.BlockSpec((tb // CHUNK, 2 * heads // QUAD, QW), lambda b, n: (tok(b, n), 0, 0)),
    ]
    args = [qkv, qkv, qkv, gb, gq]
    if s0 is not None:
        layer, direction = s0_index
        in_specs.append(pl.BlockSpec((None, None, None, heads, HEAD_DIM, HEAD_DIM),
                                     lambda b, n: (b, layer, direction, 0, 0, 0)))
        args.append(s0)
    if finalize:
        in_specs += [
            pl.BlockSpec((tb, bw), lambda b, n: (tok(b, n), 0)),
            pl.BlockSpec((tb, bw), lambda b, n: (tok(b, n), z_block)),
            pl.BlockSpec((1, HEAD_DIM), lambda b, n: (0, 0)),
        ]
        args += [o_prev, p, norm_a_l.reshape(1, HEAD_DIM)]
    in_specs, args, aliases = _with_alias(in_specs, args, out)
    out_specs = [pl.BlockSpec((tb, bw), lambda b, n: (tok(b, n), 0))]
    out_shape = [jax.ShapeDtypeStruct((m, bw), BF16 if finalize else F32)]
    if emit_state:
        out_specs.append(pl.BlockSpec((None, heads, HEAD_DIM, HEAD_DIM), lambda b, n: (b, 0, 0, 0)))
        out_shape.append(jax.ShapeDtypeStruct((n_seq, heads, HEAD_DIM, HEAD_DIM), F32))
    kern = functools.partial(_delta_kernel, heads=heads, reverse=reverse, has_s0=s0 is not None,
                             finalize=finalize, emit_state=emit_state, has_alias=out is not None)
    res = pl.pallas_call(
        kern,
        grid=(n_seq, ntb),
        in_specs=in_specs,
        out_specs=out_specs,
        out_shape=out_shape,
        scratch_shapes=[pltpu.VMEM((heads, HEAD_DIM, HEAD_DIM), F32)],
        input_output_aliases=aliases,
        compiler_params=_params("parallel", "arbitrary"),
        name="delta_scan_bwd" if reverse else "delta_scan_fwd",
    )(*args)
    return res if emit_state else (res[0], None)


def _rope(x, cosf, sins):
    return x * cosf + pltpu.roll(x, HEAD_DIM // 2, 1) * sins


def _rms_head(x, g):
    return x * lax.rsqrt(jnp.mean(x * x, axis=-1, keepdims=True) + EPS) * g


def _attn_kernel(*refs, band, has_ctx, use_sink, use_norm, use_rope, emit_k, has_alias, t_self, s_ctx, tq, n_sub,
                 grp, key_chunk):
    it = iter(refs)
    sink_ref = next(it) if use_sink else None
    q_ref, k_ref, v_ref, z_ref = (next(it) for _ in range(4))
    if has_ctx:
        kc_ref, vc_ref = next(it), next(it)
    if use_norm:
        qn_ref, kn_ref = next(it), next(it)
    if use_rope:
        cq_ref, sq_ref, ck_ref, sk_ref = (next(it) for _ in range(4))
    if has_alias:
        next(it)
    y_ref = next(it)
    kout_ref = next(it) if emit_k else None
    ks_ref, vs_ref = next(it), next(it)

    h = pl.program_id(1)
    i = pl.program_id(2)
    pad = WINDOW if band else 0
    ctx0 = t_self + 2 * pad

    @pl.when(i == 0)
    def _():
        k = k_ref[...]
        if use_norm:
            k = _rms_head(k, kn_ref[...])
        if emit_k:
            kout_ref[...] = k
        if use_rope:
            k = _rope(k, ck_ref[...], sk_ref[...])
        def ones_lane(n_rows):
            return jnp.where(lax.broadcasted_iota(jnp.int32, (n_rows, HEAD_DIM), 1) == 0, 1.0, 0.0).astype(BF16)

        ks_ref[pad:pad + t_self, :] = k.astype(BF16)
        vs_ref[pad:pad + t_self, :HEAD_DIM] = v_ref[...].astype(BF16)
        vs_ref[pad:pad + t_self, HEAD_DIM:] = ones_lane(t_self)
        if band:
            ks_ref[0:pad, :] = jnp.zeros((pad, HEAD_DIM), BF16)
            vs_ref[0:pad, :] = jnp.zeros((pad, 2 * HEAD_DIM), BF16)
            ks_ref[pad + t_self:ctx0, :] = jnp.zeros((pad, HEAD_DIM), BF16)
            vs_ref[pad + t_self:ctx0, :] = jnp.zeros((pad, 2 * HEAD_DIM), BF16)
        if has_ctx:
            ks_ref[ctx0:ctx0 + s_ctx, :] = kc_ref[...].astype(BF16)
            vs_ref[ctx0:ctx0 + s_ctx, :HEAD_DIM] = vc_ref[...].astype(BF16)
            vs_ref[ctx0:ctx0 + s_ctx, HEAD_DIM:] = ones_lane(s_ctx)

    scale = HEAD_DIM ** -0.5 * LOG2E
    rows = grp * tq
    if band:
        width = 3 * WINDOW
        krel = lax.broadcasted_iota(jnp.int32, (rows, width), 1) - WINDOW
        in_band = jnp.abs((lax.broadcasted_iota(jnp.int32, (rows, width), 0) & (tq - 1)) - krel) <= WINDOW

    def q_block(sub):
        rs = slice(sub * tq, (sub + 1) * tq)
        qi = i * n_sub + sub
        qs = []
        for g in range(grp):
            q = q_ref[rs, g * HEAD_DIM:(g + 1) * HEAD_DIM]
            if use_norm:
                q = _rms_head(q, qn_ref[...])
            if use_rope:
                q = _rope(q, cq_ref[rs, :], sq_ref[rs, :])
            qs.append((q * scale).astype(BF16))
        q16 = jnp.concatenate(qs, axis=0)

        if band:
            r0 = pl.multiple_of(qi * tq, tq)
            kpos = krel + qi * tq
            valid = in_band & (kpos >= 0) & (kpos < t_self)
            pieces = [(lambda: jnp.where(valid, _dot_nt(q16, ks_ref[pl.ds(r0, width), :]), NEG),
                       lambda: vs_ref[pl.ds(r0, width), :])]
            bounds = [(ctx0, ctx0 + s_ctx)] if has_ctx else []
        else:
            step = min(t_self, key_chunk)
            bounds = [(a, a + step) for a in range(0, t_self, step)]
            if has_ctx:
                bounds.append((t_self, t_self + s_ctx))
            pieces = []
        for a, b in bounds:
            pieces.append((lambda a=a, b=b: _dot_nt(q16, ks_ref[a:b, :]), lambda a=a, b=b: vs_ref[a:b, :]))

        if use_sink:
            rid = lax.broadcasted_iota(jnp.int32, (rows, 1), 0)
            sink = jnp.zeros((rows, 1), F32)
            for g in range(grp):
                sink = jnp.where(rid >= g * tq, sink_ref[h * grp + g] * LOG2E, sink)
            m = sink
            acc = jnp.where(lax.broadcasted_iota(jnp.int32, (rows, 2 * HEAD_DIM), 1) == HEAD_DIM, 1.0, 0.0)
        else:
            m = acc = None
        for scores, values in pieces:
            s = scores()
            m_new = jnp.max(s, axis=-1, keepdims=True)
            if m is not None:
                m_new = jnp.maximum(m, m_new)
            pv = _dot(jnp.exp2(s - m_new).astype(BF16), values())
            acc = pv if m is None else jnp.exp2(m - m_new) * acc + pv
            m = m_new
        o = acc[:, :HEAD_DIM] / acc[:, HEAD_DIM:HEAD_DIM + 1]
        for g in range(grp):
            cols = slice(g * HEAD_DIM, (g + 1) * HEAD_DIM)
            y_ref[rs, cols] = (o[g * tq:(g + 1) * tq] * _silu(z_ref[rs, cols])).astype(y_ref.dtype)

    for sub in range(n_sub):
        q_block(sub)


def _attention(p, *, q_block, k_block, v_block, z_block, seq_len, n_seq, row0, tq, heads, band, out,
               ctx_k=None, ctx_v=None, layer=None, sink_l=None, q_norm_l=None, k_norm_l=None, rope=None,
               emit_k=False):
    m = p.shape[0]
    bw = heads * HEAD_DIM
    grp = heads // KV_HEADS
    gw = grp * HEAD_DIM
    n_sub = 4
    while seq_len % (n_sub * tq):
        n_sub //= 2
    tqs = n_sub * tq
    nq = seq_len // tqs
    qblk0 = row0 // tqs
    sblk0 = row0 // seq_len
    has_ctx = ctx_k is not None
    s_ctx = ctx_k.shape[2] if has_ctx else 0
    use_sink = sink_l is not None
    use_norm = q_norm_l is not None
    use_rope = rope is not None
    pad = WINDOW if band else 0
    assert seq_len <= KEY_CHUNK or seq_len % KEY_CHUNK == 0

    in_specs, args = [], []
    if use_sink:
        in_specs.append(pl.BlockSpec(memory_space=pltpu.SMEM))
        args.append(sink_l)
    in_specs += [
        pl.BlockSpec((tqs, gw), lambda b, h, i: (qblk0 + b * nq + i, q_block + h)),
        pl.BlockSpec((seq_len, HEAD_DIM), lambda b, h, i: (sblk0 + b, k_block + h)),
        pl.BlockSpec((seq_len, HEAD_DIM), lambda b, h, i: (sblk0 + b, v_block + h)),
        pl.BlockSpec((tqs, gw), lambda b, h, i: (qblk0 + b * nq + i, z_block + h)),
    ]
    args += [p, p, p, p]
    if has_ctx:
        spec = pl.BlockSpec((None, None, s_ctx, HEAD_DIM), lambda b, h, i: (b, layer, 0, h))
        in_specs += [spec, spec]
        args += [ctx_k, ctx_v]
    if use_norm:
        spec = pl.BlockSpec((1, HEAD_DIM), lambda b, h, i: (0, 0))
        in_specs += [spec, spec]
        args += [q_norm_l.reshape(1, HEAD_DIM), k_norm_l.reshape(1, HEAD_DIM)]
    if use_rope:
        cosf, sins = rope
        qspec = pl.BlockSpec((tqs, HEAD_DIM), lambda b, h, i: (i, 0))
        kspec = pl.BlockSpec((seq_len, HEAD_DIM), lambda b, h, i: (0, 0))
        in_specs += [qspec, qspec, kspec, kspec]
        args += [cosf, sins, cosf, sins]
    in_specs, args, aliases = _with_alias(in_specs, args, out)
    out_specs = [pl.BlockSpec((tqs, gw), lambda b, h, i: (qblk0 + b * nq + i, h))]
    out_shape = [jax.ShapeDtypeStruct((m, bw), BF16)]
    if emit_k:
        out_specs.append(pl.BlockSpec((None, seq_len, HEAD_DIM), lambda b, h, i: (b, 0, h)))
        out_shape.append(jax.ShapeDtypeStruct((n_seq, seq_len, KV_W), F32))
    kern = functools.partial(_attn_kernel, band=band, has_ctx=has_ctx, use_sink=use_sink, use_norm=use_norm,
                             use_rope=use_rope, emit_k=emit_k, has_alias=out is not None, t_self=seq_len,
                             s_ctx=s_ctx, tq=tq, n_sub=n_sub, grp=grp, key_chunk=KEY_CHUNK)
    n_keys = seq_len + 2 * pad + s_ctx
    assert tq & (tq - 1) == 0 and (not band or tq == WINDOW)
    res = pl.pallas_call(
        kern,
        grid=(n_seq, KV_HEADS, nq),
        in_specs=in_specs,
        out_specs=out_specs,
        out_shape=out_shape,
        scratch_shapes=[pltpu.VMEM((n_keys, HEAD_DIM), BF16), pltpu.VMEM((n_keys, 2 * HEAD_DIM), BF16)],
        input_output_aliases=aliases,
        compiler_params=_params("parallel", "parallel", "arbitrary"),
        name=("attn_band" if band else "attn_full") + ("_ctx" if has_ctx else ""),
    )(*args)
    return res if emit_k else (res[0], None)


def _merge_kernel(ya_ref, yb_ref, yc_ref, w_ref, ga_ref, gb_ref, gc_ref, o_ref):
    acc = jax.nn.sigmoid(ga_ref[...]) * _dot(ya_ref[...], w_ref[0])
    acc = acc + jax.nn.sigmoid(gb_ref[...]) * _dot(yb_ref[...], w_ref[1])
    acc = acc + jax.nn.sigmoid(gc_ref[...]) * _dot(yc_ref[...], w_ref[2])
    o_ref[...] = acc.astype(o_ref.dtype)


def _merge(ya, yb, yc, w_branch_l, p, gate_col0, tm, tn):
    m, bw = ya.shape
    d = w_branch_l.shape[2]
    g0 = gate_col0 // tn
    gstep = d // tn
    br = pl.BlockSpec((tm, bw), lambda i, j: (i, 0))
    return pl.pallas_call(
        _merge_kernel,
        grid=(m // tm, d // tn),
        in_specs=[
            br, br, br,
            pl.BlockSpec((N_BRANCH, bw, tn), lambda i, j: (0, 0, j)),
            pl.BlockSpec((tm, tn), lambda i, j: (i, g0 + j)),
            pl.BlockSpec((tm, tn), lambda i, j: (i, g0 + gstep + j)),
            pl.BlockSpec((tm, tn), lambda i, j: (i, g0 + 2 * gstep + j)),
        ],
        out_specs=pl.BlockSpec((tm, tn), lambda i, j: (i, j)),
        out_shape=jax.ShapeDtypeStruct((m, d), BF16),
        compiler_params=_params("parallel", "arbitrary"),
        name="merge",
    )(ya, yb, yc, w_branch_l, p, p, p)


def _out_kernel(a_ref, w_ref, x_ref, mod_ref, o_ref):
    o_ref[...] = x_ref[...] + mod_ref[2:3, :] * _dot(a_ref[...], w_ref[...])


def _out_proj(mixed, w_out_l, x, mod_l, m_ctx, t_lat, ctx_row, tm, tn):
    m, d = x.shape
    return pl.pallas_call(
        _out_kernel,
        grid=(m // tm, d // tn),
        in_specs=[
            pl.BlockSpec((tm, d), lambda i, j: (i, 0)),
            pl.BlockSpec((d, tn), lambda i, j: (0, j)),
            pl.BlockSpec((tm, tn), lambda i, j: (i, j)),
            pl.BlockSpec((None, 3, tn), lambda i, j: (_mod_row(i, tm, m_ctx, t_lat, ctx_row), 0, j)),
        ],
        out_specs=pl.BlockSpec((tm, tn), lambda i, j: (i, j)),
        out_shape=jax.ShapeDtypeStruct((m, d), F32),
        compiler_params=_params("parallel", "arbitrary"),
        name="out_proj",
    )(mixed, w_out_l, x, mod_l)


def _final_kernel(x_ref, g_ref, o_ref):
    x = x_ref[...]
    o_ref[...] = x * lax.rsqrt(jnp.mean(x * x, axis=-1, keepdims=True) + EPS) * g_ref[...]


def _final_norm(x, g, row0, n_rows, tile):
    d = x.shape[1]
    blk0 = row0 // tile
    return pl.pallas_call(
        _final_kernel,
        grid=(n_rows // tile,),
        in_specs=[pl.BlockSpec((tile, d), lambda i: (blk0 + i, 0)), pl.BlockSpec((1, d), lambda i: (0, 0))],
        out_specs=pl.BlockSpec((tile, d), lambda i: (i, 0)),
        out_shape=jax.ShapeDtypeStruct((n_rows, d), F32),
        compiler_params=_params("parallel"),
        name="final_norm",
    )(x, g.reshape(1, d))


def _axial_rope_tables(n_tokens):
    rows = n_tokens // GRID_W
    row = jnp.repeat(jnp.arange(rows), GRID_W).astype(F32)
    col = jnp.tile(jnp.arange(GRID_W), rows).astype(F32)
    n_freq = HEAD_DIM // 4
    inv = ROPE_THETA ** (-jnp.arange(n_freq, dtype=F32) / n_freq)
    ang = jnp.concatenate([row[:, None] * inv, col[:, None] * inv], axis=-1)
    cos, sin = jnp.cos(ang), jnp.sin(ang)
    return jnp.concatenate([cos, cos], axis=-1), jnp.concatenate([-sin, sin], axis=-1)


def kernel(x_prompt, x_sample, state_delta, cache_win_k, cache_win_v, cache_glob_k, cache_glob_v, c, c_ctx,
           norm_g, w_mod, b_mod, w_in, conv_w, a_log, dt_bias, norm_a, sink, q_norm, k_norm, w_branch, w_out,
           final_g):
    n_layers, d = norm_g.shape
    bw = d // 2
    heads = bw // HEAD_DIM
    nbc, s_len, _ = x_prompt.shape
    nbl, t_len, _ = x_sample.shape
    past = cache_win_k.shape[2]
    m_ctx, m_lat = nbc * s_len, nbl * t_len
    m = m_ctx + m_lat
    assert m_ctx % t_len == 0 and t_len % s_len == 0 and s_len % (2 * WINDOW) == 0 and 8 * heads <= LANES
    assert heads % KV_HEADS == 0 and heads % QUAD == 0 and bw % KV_W == 0

    widths = dict(a_qkv=3 * bw, a_z=bw, b_q=bw, b_k=KV_W, b_v=KV_W, b_z=bw, c_q=bw, c_k=KV_W, c_v=KV_W, c_z=bw,
                  gates=N_BRANCH * d)
    off, col = {}, 0
    for name, wd in widths.items():
        off[name] = col
        col += wd
    n_main = col
    n_ba = 4 * heads
    src = 4 * bw

    tile_seq = s_len
    tm_big = _pick_tile(math.gcd(m_ctx, t_len), (1024, 512, 256))
    tm_merge = _pick_tile(m, (1024, 512))
    tn = _pick_tile(math.gcd(d, KV_W), (512,))
    tn_wide = _pick_tile(math.gcd(d, n_main), (1024, tn))
    tb = 4 * CHUNK
    tb_lat = _pick_tile(t_len, (8 * CHUNK, tb))

    w_in16 = w_in.astype(BF16)
    w_main = jnp.concatenate([w_in16[:, :, :src], w_in16[:, :, src + n_ba:]], axis=-1)
    w_ba = jnp.pad(w_in16[:, :, src:src + n_ba], ((0, 0), (0, 0), (0, LANES - n_ba)))
    w_branch16 = w_branch.astype(BF16)
    w_out16 = w_out.astype(BF16)

    ctx_row = nbl
    mod_rows = -(-(nbl + 1) // 8) * 8
    c_all = jnp.concatenate([c, c_ctx[None], jnp.zeros((mod_rows - nbl - 1, d), F32)], axis=0)
    mod = _modulation(c_all, w_mod, b_mod).reshape(n_layers, mod_rows, 3, d)

    rope = _axial_rope_tables(t_len)
    ckw = cache_win_k.reshape(nbl, n_layers, past, KV_W)
    cvw = cache_win_v.reshape(nbl, n_layers, past, KV_W)
    ckg = cache_glob_k.reshape(nbl, n_layers, past, KV_W)
    cvg = cache_glob_v.reshape(nbl, n_layers, past, KV_W)

    x = jnp.concatenate([x_prompt.reshape(m_ctx, d), x_sample.reshape(m_lat, d)], axis=0)
    states, win_k, win_v, glob_k, glob_v = [], [], [], [], []
    for l in range(n_layers):
        h = _prep(x, mod[l], norm_g[l], m_ctx, t_len, ctx_row, tile_seq)
        p = _matmul(h, w_main[l], tm_big, tn_wide, F32, "in_proj")
        ba = _matmul(h, w_ba[l], tm_big, LANES, F32, "in_proj_gates")

        qkv = _delta_conv(p, conv_w[l], s_len, nbc, 0, None, bw)
        qkv = _delta_conv(p, conv_w[l], t_len, nbl, m_ctx // t_len, qkv, bw)
        gb = _delta_gates(ba, a_log[l], dt_bias[l], heads, tile_seq)
        gq = jnp.swapaxes(gb.reshape(m // CHUNK, CHUNK, LANES)[:, :, 2 * heads:4 * heads], 1, 2)
        gq = gq.reshape(m // CHUNK, 2 * heads // QUAD, QW)
        ctx_scan = dict(heads=heads, tb=tb, seq_len=s_len, n_seq=nbc, row0=0)
        lat_scan = dict(heads=heads, tb=tb_lat, seq_len=t_len, n_seq=nbl, row0=m_ctx)
        fin = dict(p=p, z_block=off["a_z"] // bw, norm_a_l=norm_a[l])
        o_f, s_f = _delta_scan(qkv, gb, gq, reverse=False, emit_state=True, **ctx_scan)
        o_f, _ = _delta_scan(qkv, gb, gq, reverse=False, out=o_f, s0=state_delta, s0_index=(l, 0), **lat_scan)
        ya, s_b = _delta_scan(qkv, gb, gq, reverse=True, o_prev=o_f, emit_state=True, **ctx_scan, **fin)
        ya, _ = _delta_scan(qkv, gb, gq, reverse=True, out=ya, s0=state_delta, s0_index=(l, 1), o_prev=o_f,
                            **lat_scan, **fin)
        states.append(jnp.stack([s_f, s_b], axis=1))

        gw = (heads // KV_HEADS) * HEAD_DIM
        blocks_b = dict(q_block=off["b_q"] // gw, k_block=off["b_k"] // HEAD_DIM, v_block=off["b_v"] // HEAD_DIM,
                        z_block=off["b_z"] // gw)
        blocks_c = dict(q_block=off["c_q"] // gw, k_block=off["c_k"] // HEAD_DIM, v_block=off["c_v"] // HEAD_DIM,
                        z_block=off["c_z"] // gw)
        ctx_args = dict(seq_len=s_len, n_seq=nbc, row0=0, tq=s_len, heads=heads, band=False, out=None)
        lat_args = dict(seq_len=t_len, n_seq=nbl, row0=m_ctx, tq=WINDOW, heads=heads, layer=l, rope=rope)
        yb, _ = _attention(p, sink_l=sink[l], **blocks_b, **ctx_args)
        yb, _ = _attention(p, out=yb, sink_l=sink[l], band=True, ctx_k=ckw, ctx_v=cvw, **blocks_b, **lat_args)
        yc, kn = _attention(p, q_norm_l=q_norm[l], k_norm_l=k_norm[l], emit_k=True, **blocks_c, **ctx_args)
        yc, _ = _attention(p, out=yc, q_norm_l=q_norm[l], k_norm_l=k_norm[l], band=False, ctx_k=ckg, ctx_v=cvg,
                           **blocks_c, **lat_args)

        def ctx_slab(name):
            return p[:m_ctx, off[name]:off[name] + KV_W].reshape(nbc, s_len, KV_HEADS, HEAD_DIM)

        win_k.append(ctx_slab("b_k"))
        win_v.append(ctx_slab("b_v"))
        glob_k.append(kn.reshape(nbc, s_len, KV_HEADS, HEAD_DIM))
        glob_v.append(ctx_slab("c_v"))

        mixed = _merge(ya, yb, yc, w_branch16[l], p, off["gates"], tm_merge, tn)
        x = _out_proj(mixed, w_out16[l], x, mod[l], m_ctx, t_len, ctx_row, tm_big, tn_wide)

    y_prompt = _final_norm(x, final_g, 0, m_ctx, tile_seq).reshape(nbc, s_len, d)
    y_sample = _final_norm(x, final_g, m_ctx, m_lat, tile_seq).reshape(nbl, t_len, d)
    return (y_prompt, y_sample, jnp.stack(states, axis=1), jnp.stack(win_k, axis=1), jnp.stack(win_v, axis=1),
            jnp.stack(glob_k, axis=1), jnp.stack(glob_v, axis=1))
```
